```python
import math
import jax
import jax.numpy as jnp
from jax import lax
import numpy as np

D_MODEL = 4096
BATCH = 1
SEQ = 16384
DEPTH = 4

GRID_W = 64
CTX_LEN = 256
N_MIXERS = 2
N_MOD = 6
RMS_EPS = 1e-6
DIFF_HEAD_DIM = 128
N_HEADS = D_MODEL // (2 * DIFF_HEAD_DIM)
V_HEAD_DIM = 2 * DIFF_HEAD_DIM
ROPE_AXIS_DIM = DIFF_HEAD_DIM // 2
ROPE_THETA = 10000.0
Q_BLOCK = 128
POOL_WINDOWS = (2, 4, 8, 16)
N_POOL_GROUPS = len(POOL_WINDOWS)
POOL_GROUP_DIM = D_MODEL // N_POOL_GROUPS
N_EXPERT_GROUPS = 4
EXPERTS_PER_GROUP = 4
N_EXPERTS = N_EXPERT_GROUPS * EXPERTS_PER_GROUP
TOP_K_IN_GROUP = 2
D_FF_EXPERT = (3 * D_MODEL) // 32

kernel_name = "hybrid_pool_diffattn_hmoe_dit"


def rms_norm(x, w):
    xf = x.astype(jnp.float32)
    y = xf * lax.rsqrt(jnp.mean(xf * xf, axis=-1, keepdims=True) + RMS_EPS)
    return (y * w.astype(jnp.float32)).astype(x.dtype)


def modulate(h, shift, scale):
    return h * (1 + scale) + shift


def axial_rope_tables(n_tokens):
    rows = n_tokens // GRID_W
    row = jnp.broadcast_to(jnp.arange(rows, dtype=jnp.float32)[:, None], (rows, GRID_W)).reshape(-1)
    col = jnp.broadcast_to(jnp.arange(GRID_W, dtype=jnp.float32)[None, :], (rows, GRID_W)).reshape(-1)
    inv_freq = ROPE_THETA ** (-jnp.arange(0, ROPE_AXIS_DIM, 2, dtype=jnp.float32) / ROPE_AXIS_DIM)
    ang_r = row[:, None] * inv_freq
    ang_c = col[:, None] * inv_freq
    shape = (n_tokens, 1, 1, ROPE_AXIS_DIM // 2)
    return (jnp.cos(ang_r).reshape(shape), jnp.sin(ang_r).reshape(shape),
            jnp.cos(ang_c).reshape(shape), jnp.sin(ang_c).reshape(shape))


def _rotate(x, cos, sin):
    half = x.shape[-1] // 2
    x1, x2 = x[..., :half], x[..., half:]
    return jnp.concatenate([x1 * cos - x2 * sin, x1 * sin + x2 * cos], axis=-1)


def apply_axial_rope(x, rope):
    cos_r, sin_r, cos_c, sin_c = rope
    xf = x.astype(jnp.float32)
    xr = _rotate(xf[..., :ROPE_AXIS_DIM], cos_r, sin_r)
    xc = _rotate(xf[..., ROPE_AXIS_DIM:], cos_c, sin_c)
    return jnp.concatenate([xr, xc], axis=-1).astype(x.dtype)


def pool_mixer(h, w_pool, b_pool, pool_scale):
    B, n, _ = h.shape
    hf = h.astype(jnp.float32).reshape(B, n, N_POOL_GROUPS, POOL_GROUP_DIM)
    cs = jnp.concatenate([jnp.zeros((B, 1, N_POOL_GROUPS, POOL_GROUP_DIM), jnp.float32),
                          jnp.cumsum(hf, axis=1)], axis=1)
    t = jnp.arange(n)
    outs = []
    for g, w in enumerate(POOL_WINDOWS):
        lo = jnp.clip(t - w // 2, 0, n)
        hi = jnp.clip(t + w // 2, 0, n)
        window_sum = cs[:, hi, g] - cs[:, lo, g]
        count = (hi - lo).astype(jnp.float32)[None, :, None]
        outs.append(window_sum / count - hf[:, :, g])
    pooled = jnp.stack(outs, axis=2).astype(h.dtype)
    y = jnp.einsum('bngc,gce->bnge', pooled, w_pool) + b_pool
    return y.reshape(B, n, D_MODEL) * pool_scale


def _split_qkv(qkv):
    B, n, _ = qkv.shape
    q, k, v = jnp.split(qkv, 3, axis=-1)
    q = q.reshape(B, n, N_HEADS, 2, DIFF_HEAD_DIM)
    k = k.reshape(B, n, N_HEADS, 2, DIFF_HEAD_DIM)
    v = v.reshape(B, n, N_HEADS, V_HEAD_DIM)
    return q, k, v


def diff_softmax_attend(q, k, v, lam):
    s = jnp.einsum('bqhcd,bkhcd->bhcqk', q, k).astype(jnp.float32) * (DIFF_HEAD_DIM ** -0.5)
    p = jax.nn.softmax(s, axis=-1)
    wts = (p[:, :, 0] - lam * p[:, :, 1]).astype(v.dtype)
    return jnp.einsum('bhqk,bkhe->bqhe', wts, v)


def diff_attention(hc, hl, w_qkv, w_o, lam_vecs, subln_w, lambda_init, rope, need_ctx):
    B, L, _ = hl.shape
    qc, kc, vc = _split_qkv(hc @ w_qkv)
    ql, kl, vl = _split_qkv(hl @ w_qkv)
    ql = apply_axial_rope(ql, rope)
    kl = apply_axial_rope(kl, rope)
    lv = lam_vecs.astype(jnp.float32)
    lam = jnp.exp(jnp.sum(lv[0] * lv[1])) - jnp.exp(jnp.sum(lv[2] * lv[3])) + lambda_init
    k_all = jnp.concatenate([kc, kl], axis=1)
    v_all = jnp.concatenate([vc, vl], axis=1)
    nb = L // Q_BLOCK
    q_blocks = jnp.moveaxis(ql.reshape(B, nb, Q_BLOCK, N_HEADS, 2, DIFF_HEAD_DIM), 1, 0)
    ol = lax.map(lambda qb: diff_softmax_attend(qb, k_all, v_all, lam), q_blocks)
    ol = jnp.moveaxis(ol, 0, 1).reshape(B, L, N_HEADS, V_HEAD_DIM)

    def head_out(o):
        o = rms_norm(o, subln_w) * (1 - lambda_init)
        return o.reshape(o.shape[0], o.shape[1], D_MODEL) @ w_o

    yl = head_out(ol)
    yc = head_out(diff_softmax_attend(qc, kc, vc, lam)) if need_ctx else None
    return yc, yl


def hier_moe(h, w_rg, w_re, w1, w3, w2):
    B, n, _ = h.shape
    pg = jax.nn.softmax(jnp.einsum('bnd,dg->bng', h, w_rg).astype(jnp.float32), axis=-1)
    pg_top, g_sel = lax.top_k(pg, 1)
    g_onehot = jax.nn.one_hot(g_sel[..., 0], N_EXPERT_GROUPS, dtype=jnp.float32)
    le = jnp.einsum('bnd,de->bne', h, w_re).astype(jnp.float32).reshape(B, n, N_EXPERT_GROUPS, EXPERTS_PER_GROUP)
    le_sel = jnp.sum(le * g_onehot[..., None], axis=2)
    top_v, top_i = lax.top_k(le_sel, TOP_K_IN_GROUP)
    w_in = jax.nn.softmax(top_v, axis=-1)
    gate_in = jnp.sum(jax.nn.one_hot(top_i, EXPERTS_PER_GROUP, dtype=jnp.float32) * w_in[..., None], axis=-2)
    gate = (g_onehot[..., None] * gate_in[..., None, :] * pg_top[..., None]).reshape(B, n, N_EXPERTS)
    a = jnp.einsum('bnd,edf->bnef', h, w1)
    u = jnp.einsum('bnd,edf->bnef', h, w3)
    hid = jax.nn.silu(a) * u * gate[..., None].astype(h.dtype)
    return jnp.einsum('bnef,efd->bnd', hid, w2)


def setup_inputs(seed: int = 0) -> dict:
    key = jax.random.key(seed)
    k = jax.random.split(key, 20)
    f32 = jnp.float32
    D = D_MODEL
    n_pool = len(range(0, DEPTH, N_MIXERS))
    n_attn = len(range(1, DEPTH, N_MIXERS))

    def nrm(kk, shape, scale):
        return jax.random.normal(kk, shape, f32) * scale

    return {
        "x": nrm(k[0], (BATCH, SEQ, D), 1.0),
        "c": nrm(k[1], (BATCH, D), 1.0),
        "ctx": nrm(k[2], (BATCH, CTX_LEN, D), 1.0),
        "c_ctx": nrm(k[3], (D,), 1.0),
        "ada_w": nrm(k[4], (DEPTH, D, N_MOD * D), 0.5 * D ** -0.5),
        "ada_b": nrm(k[5], (DEPTH, N_MOD * D), 0.02),
        "norm_w": 1.0 + nrm(k[6], (DEPTH, 2, D), 0.05),
        "pool_w": nrm(k[7], (n_pool, N_POOL_GROUPS, POOL_GROUP_DIM, POOL_GROUP_DIM), POOL_GROUP_DIM ** -0.5),
        "pool_b": nrm(k[8], (n_pool, N_POOL_GROUPS, POOL_GROUP_DIM), 0.02),
        "pool_scale": 1.0 + nrm(k[9], (n_pool, D), 0.1),
        "attn_w_qkv": nrm(k[10], (n_attn, D, 3 * D), D ** -0.5),
        "attn_w_o": nrm(k[11], (n_attn, D, D), D ** -0.5),
        "attn_lambda": nrm(k[12], (n_attn, 4, DIFF_HEAD_DIM), 0.1),
        "attn_subln_w": 1.0 + nrm(k[13], (n_attn, V_HEAD_DIM), 0.05),
        "router_group_w": nrm(k[14], (DEPTH, D, N_EXPERT_GROUPS), D ** -0.5),
        "router_expert_w": nrm(k[15], (DEPTH, D, N_EXPERTS), D ** -0.5),
        "expert_w1": nrm(k[16], (DEPTH, N_EXPERTS, D, D_FF_EXPERT), D ** -0.5),
        "expert_w3": nrm(k[17], (DEPTH, N_EXPERTS, D, D_FF_EXPERT), D ** -0.5),
        "expert_w2": nrm(k[18], (DEPTH, N_EXPERTS, D_FF_EXPERT, D), D_FF_EXPERT ** -0.5),
        "final_norm_w": 1.0 + nrm(k[19], (D,), 0.05),
    }


def reference(x, c, ctx, c_ctx, ada_w, ada_b, norm_w, pool_w, pool_b, pool_scale,
              attn_w_qkv, attn_w_o, attn_lambda, attn_subln_w,
              router_group_w, router_expert_w, expert_w1, expert_w3, expert_w2, final_norm_w):
    n_lat = x.shape[1]
    rope = axial_rope_tables(n_lat)
    s_lat = jax.nn.silu(c)[:, None, :]
    s_ctx = jax.nn.silu(c_ctx)[None, None, :]
    xl, xc = x, ctx
    for i in range(DEPTH):
        last = i == DEPTH - 1
        j = i // N_MIXERS
        is_attn = i % N_MIXERS == 1
        mod_l = jnp.split(s_lat @ ada_w[i] + ada_b[i], N_MOD, axis=-1)
        mod_c = jnp.split(s_ctx @ ada_w[i] + ada_b[i], N_MOD, axis=-1)
        hl = modulate(rms_norm(xl, norm_w[i, 0]), mod_l[0], mod_l[1])
        hc = modulate(rms_norm(xc, norm_w[i, 0]), mod_c[0], mod_c[1]) if (is_attn or not last) else None
        if is_attn:
            yc, yl = diff_attention(hc, hl, attn_w_qkv[j], attn_w_o[j], attn_lambda[j], attn_subln_w[j],
                                    0.8 - 0.6 * math.exp(-0.3 * i), rope, not last)
        else:
            yl = pool_mixer(hl, pool_w[j], pool_b[j], pool_scale[j])
            yc = None if last else pool_mixer(hc, pool_w[j], pool_b[j], pool_scale[j])
        xl = xl + mod_l[2] * yl
        hl = modulate(rms_norm(xl, norm_w[i, 1]), mod_l[3], mod_l[4])
        moe_args = (router_group_w[i], router_expert_w[i], expert_w1[i], expert_w3[i], expert_w2[i])
        if last:
            xl = xl + mod_l[5] * hier_moe(hl, *moe_args)
        else:
            xc = xc + mod_c[2] * yc
            hc = modulate(rms_norm(xc, norm_w[i, 1]), mod_c[3], mod_c[4])
            n_ctx = xc.shape[1]
            y = hier_moe(jnp.concatenate([hc, hl], axis=1), *moe_args)
            xc = xc + mod_c[5] * y[:, :n_ctx]
            xl = xl + mod_l[5] * y[:, n_ctx:]
    return rms_norm(xl, final_norm_w)
```

```python
import functools
import math

import jax
import jax.numpy as jnp
from jax import lax
from jax.experimental import pallas as pl
from jax.experimental.pallas import tpu as pltpu

F32 = jnp.float32
BF16 = jnp.bfloat16

GRID_W = 64
RMS_EPS = 1e-6
ROPE_THETA = 10000.0
POOL_WINDOWS = (2, 4, 8, 16)
N_MOD = 6
N_EXPERT_GROUPS = 4
EXPERTS_PER_GROUP = 4
N_EXPERTS = N_EXPERT_GROUPS * EXPERTS_PER_GROUP
DIFF_HEAD_DIM = 128
HEAD_W = 2 * DIFF_HEAD_DIM
LANES = 128
SUBLANES = 8
MOD_CTX_ROW = 8
ROUTER_COL0 = N_EXPERT_GROUPS
VMEM_LIMIT = 56 * 1024 * 1024
NEG_BIG = -1e30
LOG2E = 1.4426950408889634


def _pick(n, candidates):
    for c in candidates:
        if c <= n and n % c == 0:
            return c
    return n


def _params(semantics):
    return pltpu.CompilerParams(dimension_semantics=semantics, vmem_limit_bytes=VMEM_LIMIT)


def _silu(v):
    return v * jax.nn.sigmoid(v)


def _adaln_kernel(c_ref, cc_ref, w_ref, b_ref, o_ref, acc_ref, *, tk, tn, nk):
    k = pl.program_id(2)

    @pl.when(k == 0)
    def _():
        acc_ref[...] = jnp.zeros_like(acc_ref)

    reps = tn // LANES

    def body(j, carry):
        a0, a1 = carry
        r = pl.multiple_of(j * SUBLANES, SUBLANES)
        w = w_ref[pl.ds(r, SUBLANES), :]
        s0 = _silu(c_ref[pl.ds(r, SUBLANES), :])
        s1 = _silu(cc_ref[pl.ds(r, SUBLANES), :])
        a0 = a0 + w * jnp.concatenate([s0] * reps, axis=1)
        a1 = a1 + w * jnp.concatenate([s1] * reps, axis=1)
        return a0, a1

    z = jnp.zeros((SUBLANES, tn), F32)
    a0, a1 = lax.fori_loop(0, tk // SUBLANES, body, (z, z))
    acc_ref[0] += a0
    acc_ref[1] += a1

    @pl.when(k == nk - 1)
    def _():
        o_ref[0:1, :] = jnp.sum(acc_ref[0], axis=0, keepdims=True) + b_ref[...]
        o_ref[1:2, :] = jnp.sum(acc_ref[1], axis=0, keepdims=True) + b_ref[...]


def _adaln(c, c_ctx, ada_w, ada_b):
    depth, d, n = ada_w.shape
    tk = _pick(d, (512, 256, 128))
    tn = _pick(n, (2048, 1024, 512, 256, 128))
    nk = d // tk
    c_b = jnp.broadcast_to(c.reshape(d, 1), (d, LANES))
    cc_b = jnp.broadcast_to(c_ctx.reshape(d, 1), (d, LANES))
    return pl.pallas_call(
        functools.partial(_adaln_kernel, tk=tk, tn=tn, nk=nk),
        grid=(depth, n // tn, nk),
        in_specs=[
            pl.BlockSpec((tk, LANES), lambda l, j, k: (k, 0)),
            pl.BlockSpec((tk, LANES), lambda l, j, k: (k, 0)),
            pl.BlockSpec((None, tk, tn), lambda l, j, k: (l, k, j)),
            pl.BlockSpec((None, 1, tn), lambda l, j, k: (l, 0, j)),
        ],
        out_specs=pl.BlockSpec((None, 2, tn), lambda l, j, k: (l, 0, j)),
        out_shape=jax.ShapeDtypeStruct((depth, 2, n), F32),
        scratch_shapes=[pltpu.VMEM((2, SUBLANES, tn), F32)],
        compiler_params=_params(("parallel", "parallel", "arbitrary")),
        name="adaln",
    )(c_b, cc_b, ada_w, ada_b.reshape(depth, 1, n))


def _mod_row(mod_ref, k, is_ctx):
    return jnp.where(is_ctx, mod_ref[MOD_CTX_ROW + k:MOD_CTX_ROW + k + 1, :], mod_ref[k:k + 1, :])


def _norm_mod(xv, w_row, shift, scale):
    r = lax.rsqrt(jnp.mean(xv * xv, axis=-1, keepdims=True) + RMS_EPS)
    return (xv * r) * (w_row * (1.0 + scale)) + shift


def _router_gates(h, h_hi, wr_ref):
    h_lo = (h - h_hi.astype(F32)).astype(BF16)
    r1 = jnp.dot(h_hi, wr_ref[...], preferred_element_type=F32)
    r2 = jnp.dot(h_lo, wr_ref[:, :LANES], preferred_element_type=F32)
    logits = r1[:, :LANES] + r1[:, LANES:] + r2
    col = lax.broadcasted_iota(jnp.int32, logits.shape, 1).astype(F32)
    far = float(4 * LANES)
    is_g = col < float(N_EXPERT_GROUPS)
    mg = jnp.max(jnp.where(is_g, logits, NEG_BIG), axis=1, keepdims=True)
    denom = jnp.sum(jnp.where(is_g, jnp.exp(logits - mg), 0.0), axis=1, keepdims=True)
    pg_top = 1.0 / denom
    g_sel = jnp.min(jnp.where(is_g & (logits == mg), col, far), axis=1, keepdims=True)
    e_lo = float(ROUTER_COL0) + float(EXPERTS_PER_GROUP) * g_sel
    in_grp = (col >= e_lo) & (col < e_lo + float(EXPERTS_PER_GROUP))
    v1 = jnp.max(jnp.where(in_grp, logits, NEG_BIG), axis=1, keepdims=True)
    i1 = jnp.min(jnp.where(in_grp & (logits == v1), col, far), axis=1, keepdims=True)
    rest = in_grp & (col != i1)
    v2 = jnp.max(jnp.where(rest, logits, NEG_BIG), axis=1, keepdims=True)
    i2 = jnp.min(jnp.where(rest & (logits == v2), col, far), axis=1, keepdims=True)
    e2 = jnp.exp(v2 - v1)
    w1 = 1.0 / (1.0 + e2)
    w2 = e2 / (1.0 + e2)
    return jnp.where(col == i1, w1, jnp.where(col == i2, w2, 0.0)) * pg_top


def _norm_kernel(*refs, has_y, router, k_shift, lat_tiles):
    it = iter(refs)
    x_ref = next(it)
    y_ref = next(it) if has_y else None
    gmod_ref = next(it) if has_y else None
    mod_ref = next(it)
    nw_ref = next(it)
    wr_ref = next(it) if router else None
    xo_ref = next(it) if has_y else None
    h_ref = next(it)
    g_ref = next(it) if router else None

    is_ctx = pl.program_id(0) >= lat_tiles
    xv = x_ref[...]
    if has_y:
        xv = xv + _mod_row(gmod_ref, 5, is_ctx) * y_ref[...]
        xo_ref[...] = xv
    h = _norm_mod(xv, nw_ref[...], _mod_row(mod_ref, k_shift, is_ctx), _mod_row(mod_ref, k_shift + 1, is_ctx))
    h_hi = h.astype(BF16)
    h_ref[...] = h_hi
    if router:
        g_ref[...] = _router_gates(h, h_hi, wr_ref)


def _norm_call(x, y, gmod, mod, nw, wr, *, k_shift, n_lat):
    t, d = x.shape
    tm = 256
    has_y = y is not None
    router = wr is not None
    row = pl.BlockSpec((tm, d), lambda i: (i, 0))
    full = lambda a: pl.BlockSpec(a.shape, lambda i: (0,) * a.ndim)
    ins, specs = [x], [row]
    if has_y:
        ins += [y, gmod]
        specs += [row, full(gmod)]
    ins += [mod, nw]
    specs += [full(mod), full(nw)]
    if router:
        ins.append(wr)
        specs.append(full(wr))
    outs, ospecs = [], []
    if has_y:
        outs.append(jax.ShapeDtypeStruct((t, d), F32))
        ospecs.append(row)
    outs.append(jax.ShapeDtypeStruct((t, d), BF16))
    ospecs.append(row)
    if router:
        outs.append(jax.ShapeDtypeStruct((t, LANES), F32))
        ospecs.append(pl.BlockSpec((tm, LANES), lambda i: (i, 0)))
    res = pl.pallas_call(
        functools.partial(_norm_kernel, has_y=has_y, router=router, k_shift=k_shift, lat_tiles=n_lat // tm),
        grid=(t // tm,),
        in_specs=specs,
        out_specs=ospecs,
        out_shape=outs,
        compiler_params=_params(("parallel",)),
        name="norm_mod",
    )(*ins)
    res = list(res)
    xo = res.pop(0) if has_y else x
    h = res.pop(0)
    g = res.pop(0) if router else None
    return xo, h, g


HALO = 8


def _pool_kernel(*refs, has_y, tp, cg, lat_tiles, n_tiles):
    it = iter(refs)
    x_ref, xp_ref, xn_ref = next(it), next(it), next(it)
    if has_y:
        y_ref, yp_ref, yn_ref, gmod_ref = next(it), next(it), next(it), next(it)
    mod_ref, nw_ref, pw_ref, pb_ref, ps_ref, wr_ref = (next(it) for _ in range(6))
    x1_ref, h2_ref, g_ref = next(it), next(it), next(it)
    hext_ref, xin_ref = next(it), next(it)

    j = pl.program_id(0)
    is_ctx = j >= lat_tiles
    first = (j == 0) | (j == lat_tiles)
    last = (j == lat_tiles - 1) | (j == n_tiles - 1)

    nw = nw_ref[0:1, :]
    ws = nw * (1.0 + _mod_row(mod_ref, 1, is_ctx))
    shift = _mod_row(mod_ref, 0, is_ctx)

    def x_in(xr, yr):
        xv = xr[...]
        if has_y:
            xv = xv + _mod_row(gmod_ref, 5, is_ctx) * yr[...]
        return xv

    def h_of(xv):
        r = lax.rsqrt(jnp.mean(xv * xv, axis=-1, keepdims=True) + RMS_EPS)
        return (xv * r) * ws + shift

    xm = x_in(x_ref, y_ref if has_y else None)
    xin_ref[...] = xm
    hext_ref[HALO:HALO + tp, :] = h_of(xm)
    hp = h_of(x_in(xp_ref, yp_ref if has_y else None))
    hext_ref[0:HALO, :] = jnp.where(first, 0.0, hp)
    hn = h_of(x_in(xn_ref, yn_ref if has_y else None))
    hext_ref[HALO + tp:HALO + tp + HALO, :] = jnp.where(last, 0.0, hn)

    tpos = lax.broadcasted_iota(jnp.int32, (tp, 1), 0).astype(F32)
    lo_lim = jnp.where(first, 0.0, -float(2 * HALO))
    hi_lim = jnp.where(last, float(tp), float(tp + 2 * HALO))
    gate = _mod_row(mod_ref, 2, is_ctx)
    for g, w in enumerate(POOL_WINDOWS):
        c0 = g * cg
        half = w // 2
        wsum = hext_ref[pl.ds(HALO - half, tp), c0:c0 + cg]
        for dd in range(-half + 1, half):
            wsum = wsum + hext_ref[pl.ds(HALO + dd, tp), c0:c0 + cg]
        cnt = jnp.minimum(tpos + float(half), hi_lim) - jnp.maximum(tpos - float(half), lo_lim)
        pooled = wsum / cnt - hext_ref[HALO:HALO + tp, c0:c0 + cg]
        yg = jnp.dot(pooled.astype(BF16), pw_ref[g], preferred_element_type=F32) + pb_ref[g:g + 1, :]
        yg = yg * ps_ref[:, c0:c0 + cg]
        x1_ref[:, c0:c0 + cg] = xin_ref[:, c0:c0 + cg] + gate[:, c0:c0 + cg] * yg

    x1 = x1_ref[...]
    h2 = _norm_mod(x1, nw_ref[1:2, :], _mod_row(mod_ref, 3, is_ctx), _mod_row(mod_ref, 4, is_ctx))
    h2_hi = h2.astype(BF16)
    h2_ref[...] = h2_hi
    g_ref[...] = _router_gates(h2, h2_hi, wr_ref)


def _pool_call(x, y, gmod, mod, nw2, pw, pb, ps, wr, *, n_lat):
    t, d = x.shape
    tp = 128
    cg = d // len(POOL_WINDOWS)
    has_y = y is not None
    n_tiles = t // tp
    bpt = tp // HALO
    last_blk = t // HALO - 1
    row = pl.BlockSpec((tp, d), lambda i: (i, 0))
    prev = pl.BlockSpec((HALO, d), lambda i: (jnp.maximum(i * bpt - 1, 0), 0))
    nxt = pl.BlockSpec((HALO, d), lambda i: (jnp.minimum((i + 1) * bpt, last_blk), 0))
    full = lambda a: pl.BlockSpec(a.shape, lambda i: (0,) * a.ndim)
    ins, specs = [x, x, x], [row, prev, nxt]
    if has_y:
        ins += [y, y, y, gmod]
        specs += [row, prev, nxt, full(gmod)]
    ins += [mod, nw2, pw, pb, ps, wr]
    specs += [full(mod), full(nw2), full(pw), full(pb), full(ps), full(wr)]
    return pl.pallas_call(
        functools.partial(_pool_kernel, has_y=has_y, tp=tp, cg=cg, lat_tiles=n_lat // tp, n_tiles=n_tiles),
        grid=(n_tiles,),
        in_specs=specs,
        out_specs=[row, row, pl.BlockSpec((tp, LANES), lambda i: (i, 0))],
        out_shape=[jax.ShapeDtypeStruct((t, d), F32), jax.ShapeDtypeStruct((t, d), BF16),
                   jax.ShapeDtypeStruct((t, LANES), F32)],
        scratch_shapes=[pltpu.VMEM((tp + 2 * HALO, d), F32), pltpu.VMEM((tp, d), F32)],
        compiler_params=_params(("parallel",)),
        name="pool_layer",
    )(*ins)


def _qkv_kernel(a_ref, w_ref, c_ref, sa_ref, sb_ref, o_ref, *, d, tn, qscale):
    j = pl.program_id(1)
    acc = jnp.dot(a_ref[...], w_ref[...], preferred_element_type=F32)
    n_rot = (2 * d) // tn
    n_q = d // tn

    @pl.when(j < n_rot)
    def _():
        f = jnp.where(j < n_q, qscale, 1.0)
        cc = c_ref[...] * f
        sa = sa_ref[...] * f
        sb = sb_ref[...] * f
        for b in range(tn // LANES):
            blk = acc[:, b * LANES:(b + 1) * LANES]
            rot = blk * cc + pltpu.roll(blk, 96, 1) * sa + pltpu.roll(blk, 32, 1) * sb
            o_ref[:, b * LANES:(b + 1) * LANES] = rot.astype(BF16)

    @pl.when(j >= n_rot)
    def _():
        o_ref[...] = acc.astype(BF16)


def _qkv_call(h, w, rope_c, rope_sa, rope_sb):
    t, d = h.shape
    n = w.shape[1]
    tm = _pick(t, (640, 512, 256, 128))
    tn = _pick(d, (1024, 512, 256))
    qscale = (DIFF_HEAD_DIM ** -0.5) * LOG2E
    tab = pl.BlockSpec((tm, LANES), lambda i, j: (i, 0))
    return pl.pallas_call(
        functools.partial(_qkv_kernel, d=d, tn=tn, qscale=qscale),
        grid=(t // tm, n // tn),
        in_specs=[pl.BlockSpec((tm, d), lambda i, j: (i, 0)), pl.BlockSpec((d, tn), lambda i, j: (0, j)),
                  tab, tab, tab],
        out_specs=pl.BlockSpec((tm, tn), lambda i, j: (i, j)),
        out_shape=jax.ShapeDtypeStruct((t, n), BF16),
        compiler_params=_params(("parallel", "arbitrary")),
        name="qkv_proj",
    )(h, w, rope_c, rope_sa, rope_sb)


def _oproj_kernel(a_ref, w_ref, x_ref, mod_ref, o_ref, *, tm, n_lat):
    acc = jnp.dot(a_ref[...], w_ref[...], preferred_element_type=F32)
    rows = pl.program_id(0) * tm + lax.broadcasted_iota(jnp.int32, (tm, 1), 0)
    gate = jnp.where(rows >= n_lat, mod_ref[MOD_CTX_ROW + 2:MOD_CTX_ROW + 3, :], mod_ref[2:3, :])
    o_ref[...] = x_ref[...] + gate * acc


def _oproj_call(o, w, x, mod, *, n_lat):
    t, d = o.shape
    tm = _pick(t, (640, 512, 256, 128))
    tn = _pick(d, (1024, 512, 256))
    return pl.pallas_call(
        functools.partial(_oproj_kernel, tm=tm, n_lat=n_lat),
        grid=(t // tm, d // tn),
        in_specs=[pl.BlockSpec((tm, d), lambda i, j: (i, 0)), pl.BlockSpec((d, tn), lambda i, j: (0, j)),
                  pl.BlockSpec((tm, tn), lambda i, j: (i, j)), pl.BlockSpec((16, tn), lambda i, j: (0, j))],
        out_specs=pl.BlockSpec((tm, tn), lambda i, j: (i, j)),
        out_shape=jax.ShapeDtypeStruct((t, d), F32),
        compiler_params=_params(("parallel", "arbitrary")),
        name="attn_out_proj",
    )(o, w, x, mod)


def _diff_lambda(lam_ref, lambda_init):
    lv = lam_ref[...]
    a = jnp.sum(lv[0:1, :] * lv[1:2, :], axis=1, keepdims=True)
    b = jnp.sum(lv[2:3, :] * lv[3:4, :], axis=1, keepdims=True)
    return jnp.exp(a) - jnp.exp(b) + lambda_init


def _subln(o, sw_ref, lambda_init):
    r = lax.rsqrt(jnp.mean(o * o, axis=-1, keepdims=True) + RMS_EPS)
    return ((o * r) * sw_ref[...]) * (1.0 - lambda_init)


def _attn_kernel(q_ref, k_ref, v_ref, kc_ref, vc_ref, lam_ref, sw_ref, o_ref, m_ref, l_ref, acc_ref,
                 *, lambda_init, nk):
    ki = pl.program_id(2)

    def step(kk, vv):
        for c in range(2):
            qc = q_ref[:, c * DIFF_HEAD_DIM:(c + 1) * DIFF_HEAD_DIM]
            kc = kk[:, c * DIFF_HEAD_DIM:(c + 1) * DIFF_HEAD_DIM]
            s = lax.dot_general(qc, kc, (((1,), (1,)), ((), ())), preferred_element_type=F32)
            m_prev = m_ref[c]
            m_new = jnp.maximum(m_prev, jnp.max(s, axis=1, keepdims=True))
            alpha = jnp.exp2(m_prev - m_new)
            p = jnp.exp2(s - m_new)
            l_ref[c] = alpha * l_ref[c] + jnp.sum(p, axis=1, keepdims=True)
            acc_ref[c] = alpha * acc_ref[c] + jnp.dot(p.astype(BF16), vv, preferred_element_type=F32)
            m_ref[c] = m_new

    @pl.when(ki == 0)
    def _():
        m_ref[...] = jnp.full(m_ref.shape, NEG_BIG, F32)
        l_ref[...] = jnp.zeros_like(l_ref)
        acc_ref[...] = jnp.zeros_like(acc_ref)
        step(kc_ref[...], vc_ref[...])

    step(k_ref[...], v_ref[...])

    @pl.when(ki == nk - 1)
    def _():
        lam = _diff_lambda(lam_ref, lambda_init)
        o = acc_ref[0] / l_ref[0] - lam * (acc_ref[1] / l_ref[1])
        o_ref[...] = _subln(o, sw_ref, lambda_init).astype(BF16)


def _attn_ctx_kernel(q_ref, k_ref, v_ref, lam_ref, sw_ref, oin_ref, o_ref, *, lambda_init):
    del oin_ref
    lam = _diff_lambda(lam_ref, lambda_init)
    outs = []
    for c in range(2):
        qc = q_ref[:, c * DIFF_HEAD_DIM:(c + 1) * DIFF_HEAD_DIM]
        kc = k_ref[:, c * DIFF_HEAD_DIM:(c + 1) * DIFF_HEAD_DIM]
        s = lax.dot_general(qc, kc, (((1,), (1,)), ((), ())), preferred_element_type=F32)
        p = jnp.exp2(s - jnp.max(s, axis=1, keepdims=True))
        l = jnp.sum(p, axis=1, keepdims=True)
        outs.append(jnp.dot(p.astype(BF16), v_ref[...], preferred_element_type=F32) / l)
    o_ref[...] = _subln(outs[0] - lam * outs[1], sw_ref, lambda_init).astype(BF16)


def _attn_call(qkv, lam_vecs, subln_w, *, lambda_init, n_lat, n_ctx):
    t, n3 = qkv.shape
    d = n3 // 3
    heads = d // HEAD_W
    tq = _pick(n_lat, (512, 256, 128))
    tk = _pick(n_lat, (1024, 512, 256, 128))
    nk = n_lat // tk
    cblk = n_lat // n_ctx
    sw = subln_w.reshape(1, HEAD_W)
    small = lambda a: pl.BlockSpec(a.shape, lambda *_: (0,) * a.ndim)
    o_lat = pl.pallas_call(
        functools.partial(_attn_kernel, lambda_init=lambda_init, nk=nk),
        grid=(heads, n_lat // tq, nk),
        in_specs=[
            pl.BlockSpec((tq, HEAD_W), lambda h, i, k: (i, h)),
            pl.BlockSpec((tk, HEAD_W), lambda h, i, k: (k, heads + h)),
            pl.BlockSpec((tk, HEAD_W), lambda h, i, k: (k, 2 * heads + h)),
            pl.BlockSpec((n_ctx, HEAD_W), lambda h, i, k: (cblk, heads + h)),
            pl.BlockSpec((n_ctx, HEAD_W), lambda h, i, k: (cblk, 2 * heads + h)),
            small(lam_vecs), small(sw),
        ],
        out_specs=pl.BlockSpec((tq, HEAD_W), lambda h, i, k: (i, h)),
        out_shape=jax.ShapeDtypeStruct((t, d), BF16),
        scratch_shapes=[pltpu.VMEM((2, tq, 1), F32), pltpu.VMEM((2, tq, 1), F32),
                        pltpu.VMEM((2, tq, HEAD_W), F32)],
        compiler_params=_params(("parallel", "parallel", "arbitrary")),
        name="diff_attn",
    )(qkv, qkv, qkv, qkv, qkv, lam_vecs, sw)
    return pl.pallas_call(
        functools.partial(_attn_ctx_kernel, lambda_init=lambda_init),
        grid=(heads,),
        in_specs=[
            pl.BlockSpec((n_ctx, HEAD_W), lambda h: (cblk, h)),
            pl.BlockSpec((n_ctx, HEAD_W), lambda h: (cblk, heads + h)),
            pl.BlockSpec((n_ctx, HEAD_W), lambda h: (cblk, 2 * heads + h)),
            small(lam_vecs), small(sw),
            pl.BlockSpec(memory_space=pl.ANY),
        ],
        out_specs=pl.BlockSpec((n_ctx, HEAD_W), lambda h: (cblk, h)),
        out_shape=jax.ShapeDtypeStruct((t, d), BF16),
        input_output_aliases={5: 0},
        compiler_params=_params(("arbitrary",)),
        name="diff_attn_ctx",
    )(qkv, qkv, qkv, lam_vecs, sw, o_lat)


def _moe_kernel(h_ref, g_ref, w1_ref, w3_ref, w2_ref, y_ref):
    e = pl.program_id(1)

    @pl.when(e == 0)
    def _():
        y_ref[...] = jnp.zeros_like(y_ref)

    hv = h_ref[...]
    a = jnp.dot(hv, w1_ref[...], preferred_element_type=F32)
    u = jnp.dot(hv, w3_ref[...], preferred_element_type=F32)
    gt = g_ref[...]
    col = lax.broadcasted_iota(jnp.int32, gt.shape, 1)
    ge = jnp.sum(jnp.where(col == e + ROUTER_COL0, gt, 0.0), axis=1, keepdims=True)
    hid = (_silu(a) * u) * ge
    y_ref[...] += jnp.dot(hid.astype(BF16), w2_ref[...], preferred_element_type=F32)


def _moe_call(h, gates, w1, w3, w2):
    t, d = h.shape
    n_e, _, f = w1.shape
    tm = _pick(t, (640, 512, 256, 128))
    return pl.pallas_call(
        _moe_kernel,
        grid=(t // tm, n_e),
        in_specs=[
            pl.BlockSpec((tm, d), lambda i, e: (i, 0)),
            pl.BlockSpec((tm, LANES), lambda i, e: (i, 0)),
            pl.BlockSpec((None, d, f), lambda i, e: (e, 0, 0)),
            pl.BlockSpec((None, d, f), lambda i, e: (e, 0, 0)),
            pl.BlockSpec((None, f, d), lambda i, e: (e, 0, 0)),
        ],
        out_specs=pl.BlockSpec((tm, d), lambda i, e: (i, 0)),
        out_shape=jax.ShapeDtypeStruct((t, d), F32),
        compiler_params=_params(("parallel", "arbitrary")),
        name="moe_dense",
    )(h, gates, w1, w3, w2)


def _final_kernel(x_ref, y_ref, gmod_ref, fw_ref, o_ref):
    xv = x_ref[...] + gmod_ref[5:6, :] * y_ref[...]
    r = lax.rsqrt(jnp.mean(xv * xv, axis=-1, keepdims=True) + RMS_EPS)
    o_ref[...] = (xv * r) * fw_ref[...]


def _final_call(x, y, gmod, fw, *, n_lat):
    _, d = x.shape
    tm = 256
    row = pl.BlockSpec((tm, d), lambda i: (i, 0))
    full = lambda a: pl.BlockSpec(a.shape, lambda i: (0,) * a.ndim)
    return pl.pallas_call(
        _final_kernel,
        grid=(n_lat // tm,),
        in_specs=[row, row, full(gmod), full(fw)],
        out_specs=row,
        out_shape=jax.ShapeDtypeStruct((n_lat, d), F32),
        compiler_params=_params(("parallel",)),
        name="final_norm",
    )(x, y, gmod, fw)


def _rope_tables(n_lat, n_ctx):
    axis_dim = DIFF_HEAD_DIM // 2
    quarter = axis_dim // 2
    rows = n_lat // GRID_W
    row = jnp.broadcast_to(jnp.arange(rows, dtype=F32)[:, None], (rows, GRID_W)).reshape(-1)
    col = jnp.broadcast_to(jnp.arange(GRID_W, dtype=F32)[None, :], (rows, GRID_W)).reshape(-1)
    inv_freq = ROPE_THETA ** (-jnp.arange(0, axis_dim, 2, dtype=F32) / axis_dim)
    ang_r = row[:, None] * inv_freq
    ang_c = col[:, None] * inv_freq
    zeros = jnp.zeros((n_lat, quarter), F32)
    c = jnp.concatenate([jnp.cos(ang_r), jnp.cos(ang_r), jnp.cos(ang_c), jnp.cos(ang_c)], axis=1)
    sa = jnp.concatenate([-jnp.sin(ang_r), zeros, -jnp.sin(ang_c), zeros], axis=1)
    sb = jnp.concatenate([zeros, jnp.sin(ang_r), zeros, jnp.sin(ang_c)], axis=1)
    pad = lambda a, v: jnp.concatenate([a, jnp.full((n_ctx, DIFF_HEAD_DIM), v, F32)], axis=0)
    return pad(c, 1.0), pad(sa, 0.0), pad(sb, 0.0)


def _pack_mod(m, d):
    z = jnp.zeros((MOD_CTX_ROW - N_MOD, d), F32)
    return jnp.concatenate([m[0].reshape(N_MOD, d), z, m[1].reshape(N_MOD, d), z], axis=0)


def kernel(x, c, ctx, c_ctx, ada_w, ada_b, norm_w, pool_w, pool_b, pool_scale, attn_w_qkv, attn_w_o,
           attn_lambda, attn_subln_w, router_group_w, router_expert_w, expert_w1, expert_w3, expert_w2,
           final_norm_w):
    _, n_lat, d = x.shape
    n_ctx = ctx.shape[1]
    depth = ada_w.shape[0]
    assert x.shape[0] == 1 and n_lat % n_ctx == 0 and n_ctx % 256 == 0 and d % (4 * LANES) == 0

    xs = jnp.concatenate([x[0], ctx[0]], axis=0)
    mods = _adaln(c, c_ctx, ada_w, ada_b)
    mod = [_pack_mod(mods[i], d) for i in range(depth)]

    wr = jnp.concatenate([router_group_w, router_expert_w], axis=-1)
    wr = jnp.pad(wr, ((0, 0), (0, 0), (0, LANES - wr.shape[-1])))
    wr_hi = wr.astype(BF16)
    wr_lo = (wr - wr_hi.astype(F32)).astype(BF16)
    wr2 = jnp.concatenate([wr_hi, wr_lo], axis=-1)

    pool_w16 = pool_w.astype(BF16)
    wqkv16 = attn_w_qkv.astype(BF16)
    wo16 = attn_w_o.astype(BF16)
    w1_16, w3_16, w2_16 = expert_w1.astype(BF16), expert_w3.astype(BF16), expert_w2.astype(BF16)
    rope_c, rope_sa, rope_sb = _rope_tables(n_lat, n_ctx)

    y = None
    for i in range(depth):
        jm = i // 2
        gprev = mod[i - 1] if i > 0 else None
        if i % 2 == 0:
            xs, h2, gates = _pool_call(xs, y, gprev, mod[i], norm_w[i], pool_w16[jm], pool_b[jm],
                                       pool_scale[jm].reshape(1, d), wr2[i], n_lat=n_lat)
        else:
            lambda_init = 0.8 - 0.6 * math.exp(-0.3 * i)
            xs, h, _ = _norm_call(xs, y, gprev, mod[i], norm_w[i, 0:1], None, k_shift=0, n_lat=n_lat)
            qkv = _qkv_call(h, wqkv16[jm], rope_c, rope_sa, rope_sb)
            o = _attn_call(qkv, attn_lambda[jm], attn_subln_w[jm], lambda_init=lambda_init,
                           n_lat=n_lat, n_ctx=n_ctx)
            xs = _oproj_call(o, wo16[jm], xs, mod[i], n_lat=n_lat)
            _, h2, gates = _norm_call(xs, None, None, mod[i], norm_w[i, 1:2], wr2[i], k_shift=3, n_lat=n_lat)
        y = _moe_call(h2, gates, w1_16[i], w3_16[i], w2_16[i])
    out = _final_call(xs, y, mod[depth - 1], final_norm_w.reshape(1, d), n_lat=n_lat)
    return out[None]
```

```python
import functools
import math

import jax
import jax.numpy as jnp
from jax import lax
from jax.experimental import pallas as pl
from jax.experimental.pallas import tpu as pltpu

F32 = jnp.float32
BF16 = jnp.bfloat16

GRID_W = 64
RMS_EPS = 1e-6
ROPE_THETA = 10000.0
POOL_WINDOWS = (2, 4, 8, 16)
N_MOD = 6
N_EXPERT_GROUPS = 4
EXPERTS_PER_GROUP = 4
N_EXPERTS = N_EXPERT_GROUPS * EXPERTS_PER_GROUP
DIFF_HEAD_DIM = 128
HEAD_W = 2 * DIFF_HEAD_DIM
LANES = 128
SUBLANES = 8
MOD_CTX_ROW = 8
ROUTER_COL0 = N_EXPERT_GROUPS
VMEM_LIMIT = 56 * 1024 * 1024
NEG_BIG = -1e30
LOG2E = 1.4426950408889634


def _pick(n, candidates):
    for c in candidates:
        if c <= n and n % c == 0:
            return c
    return n


def _params(semantics):
    return pltpu.CompilerParams(dimension_semantics=semantics, vmem_limit_bytes=VMEM_LIMIT)


def _silu(v):
    return v * jax.nn.sigmoid(v)


def _adaln_kernel(c_ref, cc_ref, w_ref, b_ref, o_ref, acc_ref, *, tk, tn, nk):
    k = pl.program_id(2)

    @pl.when(k == 0)
    def _():
        acc_ref[...] = jnp.zeros_like(acc_ref)

    reps = tn // LANES

    def body(j, carry):
        a0, a1 = carry
        r = pl.multiple_of(j * SUBLANES, SUBLANES)
        w = w_ref[pl.ds(r, SUBLANES), :]
        s0 = _silu(c_ref[pl.ds(r, SUBLANES), :])
        s1 = _silu(cc_ref[pl.ds(r, SUBLANES), :])
        a0 = a0 + w * jnp.concatenate([s0] * reps, axis=1)
        a1 = a1 + w * jnp.concatenate([s1] * reps, axis=1)
        return a0, a1

    z = jnp.zeros((SUBLANES, tn), F32)
    a0, a1 = lax.fori_loop(0, tk // SUBLANES, body, (z, z))
    acc_ref[0] += a0
    acc_ref[1] += a1

    @pl.when(k == nk - 1)
    def _():
        o_ref[0:1, :] = jnp.sum(acc_ref[0], axis=0, keepdims=True) + b_ref[...]
        o_ref[1:2, :] = jnp.sum(acc_ref[1], axis=0, keepdims=True) + b_ref[...]


def _adaln(c, c_ctx, ada_w, ada_b):
    depth, d, n = ada_w.shape
    tk = _pick(d, (512, 256, 128))
    tn = _pick(n, (2048, 1024, 512, 256, 128))
    nk = d // tk
    c_b = jnp.broadcast_to(c.reshape(d, 1), (d, LANES))
    cc_b = jnp.broadcast_to(c_ctx.reshape(d, 1), (d, LANES))
    return pl.pallas_call(
        functools.partial(_adaln_kernel, tk=tk, tn=tn, nk=nk),
        grid=(depth, n // tn, nk),
        in_specs=[
            pl.BlockSpec((tk, LANES), lambda l, j, k: (k, 0)),
            pl.BlockSpec((tk, LANES), lambda l, j, k: (k, 0)),
            pl.BlockSpec((None, tk, tn), lambda l, j, k: (l, k, j)),
            pl.BlockSpec((None, 1, tn), lambda l, j, k: (l, 0, j)),
        ],
        out_specs=pl.BlockSpec((None, 2, tn), lambda l, j, k: (l, 0, j)),
        out_shape=jax.ShapeDtypeStruct((depth, 2, n), F32),
        scratch_shapes=[pltpu.VMEM((2, SUBLANES, tn), F32)],
        compiler_params=_params(("parallel", "parallel", "arbitrary")),
        name="adaln",
    )(c_b, cc_b, ada_w, ada_b.reshape(depth, 1, n))


def _mod_row(mod_ref, k, is_ctx):
    return jnp.where(is_ctx, mod_ref[MOD_CTX_ROW + k:MOD_CTX_ROW + k + 1, :], mod_ref[k:k + 1, :])


def _norm_mod(xv, w_row, shift, scale):
    r = lax.rsqrt(jnp.mean(xv * xv, axis=-1, keepdims=True) + RMS_EPS)
    return (xv * r) * (w_row * (1.0 + scale)) + shift


def _router_gates(h, h_hi, wr_ref):
    h_lo = (h - h_hi.astype(F32)).astype(BF16)
    r1 = jnp.dot(h_hi, wr_ref[...], preferred_element_type=F32)
    r2 = jnp.dot(h_lo, wr_ref[:, :LANES], preferred_element_type=F32)
    logits = r1[:, :LANES] + r1[:, LANES:] + r2
    col = lax.broadcasted_iota(jnp.int32, logits.shape, 1).astype(F32)
    far = float(4 * LANES)
    is_g = col < float(N_EXPERT_GROUPS)
    mg = jnp.max(jnp.where(is_g, logits, NEG_BIG), axis=1, keepdims=True)
    denom = jnp.sum(jnp.where(is_g, jnp.exp(logits - mg), 0.0), axis=1, keepdims=True)
    pg_top = 1.0 / denom
    g_sel = jnp.min(jnp.where(is_g & (logits == mg), col, far), axis=1, keepdims=True)
    e_lo = float(ROUTER_COL0) + float(EXPERTS_PER_GROUP) * g_sel
    in_grp = (col >= e_lo) & (col < e_lo + float(EXPERTS_PER_GROUP))
    v1 = jnp.max(jnp.where(in_grp, logits, NEG_BIG), axis=1, keepdims=True)
    i1 = jnp.min(jnp.where(in_grp & (logits == v1), col, far), axis=1, keepdims=True)
    rest = in_grp & (col != i1)
    v2 = jnp.max(jnp.where(rest, logits, NEG_BIG), axis=1, keepdims=True)
    i2 = jnp.min(jnp.where(rest & (logits == v2), col, far), axis=1, keepdims=True)
    e2 = jnp.exp(v2 - v1)
    w1 = 1.0 / (1.0 + e2)
    w2 = e2 / (1.0 + e2)
    return jnp.where(col == i1, w1, jnp.where(col == i2, w2, 0.0)) * pg_top


def _norm_kernel(*refs, has_y, router, k_shift, lat_tiles):
    it = iter(refs)
    x_ref = next(it)
    y_ref = next(it) if has_y else None
    gmod_ref = next(it) if has_y else None
    mod_ref = next(it)
    nw_ref = next(it)
    wr_ref = next(it) if router else None
    xo_ref = next(it) if has_y else None
    h_ref = next(it)
    g_ref = next(it) if router else None

    is_ctx = pl.program_id(0) >= lat_tiles
    xv = x_ref[...]
    if has_y:
        xv = xv + _mod_row(gmod_ref, 5, is_ctx) * y_ref[...]
        xo_ref[...] = xv
    h = _norm_mod(xv, nw_ref[...], _mod_row(mod_ref, k_shift, is_ctx), _mod_row(mod_ref, k_shift + 1, is_ctx))
    h_hi = h.astype(BF16)
    h_ref[...] = h_hi
    if router:
        g_ref[...] = _router_gates(h, h_hi, wr_ref)


def _norm_call(x, y, gmod, mod, nw, wr, *, k_shift, n_lat):
    t, d = x.shape
    tm = 256
    has_y = y is not None
    router = wr is not None
    row = pl.BlockSpec((tm, d), lambda i: (i, 0))
    full = lambda a: pl.BlockSpec(a.shape, lambda i: (0,) * a.ndim)
    ins, specs = [x], [row]
    if has_y:
        ins += [y, gmod]
        specs += [row, full(gmod)]
    ins += [mod, nw]
    specs += [full(mod), full(nw)]
    if router:
        ins.append(wr)
        specs.append(full(wr))
    outs, ospecs = [], []
    if has_y:
        outs.append(jax.ShapeDtypeStruct((t, d), F32))
        ospecs.append(row)
    outs.append(jax.ShapeDtypeStruct((t, d), BF16))
    ospecs.append(row)
    if router:
        outs.append(jax.ShapeDtypeStruct((t, LANES), F32))
        ospecs.append(pl.BlockSpec((tm, LANES), lambda i: (i, 0)))
    res = pl.pallas_call(
        functools.partial(_norm_kernel, has_y=has_y, router=router, k_shift=k_shift, lat_tiles=n_lat // tm),
        grid=(t // tm,),
        in_specs=specs,
        out_specs=ospecs,
        out_shape=outs,
        compiler_params=_params(("parallel",)),
        name="norm_mod",
    )(*ins)
    res = list(res)
    xo = res.pop(0) if has_y else x
    h = res.pop(0)
    g = res.pop(0) if router else None
    return xo, h, g


HALO = 8


def _pool_kernel(*refs, has_y, tp, cg, lat_tiles, n_tiles):
    it = iter(refs)
    x_ref, xp_ref, xn_ref = next(it), next(it), next(it)
    if has_y:
        y_ref, yp_ref, yn_ref, gmod_ref = next(it), next(it), next(it), next(it)
    mod_ref, nw_ref, pw_ref, pb_ref, ps_ref, wr_ref = (next(it) for _ in range(6))
    x1_ref, h2_ref, g_ref = next(it), next(it), next(it)
    hext_ref, xin_ref = next(it), next(it)

    j = pl.program_id(0)
    is_ctx = j >= lat_tiles
    first = (j == 0) | (j == lat_tiles)
    last = (j == lat_tiles - 1) | (j == n_tiles - 1)

    nw = nw_ref[0:1, :]
    ws = nw * (1.0 + _mod_row(mod_ref, 1, is_ctx))
    shift = _mod_row(mod_ref, 0, is_ctx)

    def x_in(xr, yr):
        xv = xr[...]
        if has_y:
            xv = xv + _mod_row(gmod_ref, 5, is_ctx) * yr[...]
        return xv

    def h_of(xv):
        r = lax.rsqrt(jnp.mean(xv * xv, axis=-1, keepdims=True) + RMS_EPS)
        return (xv * r) * ws + shift

    xm = x_in(x_ref, y_ref if has_y else None)
    xin_ref[...] = xm
    hext_ref[HALO:HALO + tp, :] = h_of(xm)
    hp = h_of(x_in(xp_ref, yp_ref if has_y else None))
    hext_ref[0:HALO, :] = jnp.where(first, 0.0, hp)
    hn = h_of(x_in(xn_ref, yn_ref if has_y else None))
    hext_ref[HALO + tp:HALO + tp + HALO, :] = jnp.where(last, 0.0, hn)

    tpos = lax.broadcasted_iota(jnp.int32, (tp, 1), 0).astype(F32)
    lo_lim = jnp.where(first, 0.0, -float(2 * HALO))
    hi_lim = jnp.where(last, float(tp), float(tp + 2 * HALO))
    gate = _mod_row(mod_ref, 2, is_ctx)
    for g, w in enumerate(POOL_WINDOWS):
        c0 = g * cg
        half = w // 2
        wsum = hext_ref[pl.ds(HALO - half, tp), c0:c0 + cg]
        for dd in range(-half + 1, half):
            wsum = wsum + hext_ref[pl.ds(HALO + dd, tp), c0:c0 + cg]
        cnt = jnp.minimum(tpos + float(half), hi_lim) - jnp.maximum(tpos - float(half), lo_lim)
        pooled = wsum / cnt - hext_ref[HALO:HALO + tp, c0:c0 + cg]
        yg = jnp.dot(pooled.astype(BF16), pw_ref[g], preferred_element_type=F32) + pb_ref[g:g + 1, :]
        yg = yg * ps_ref[:, c0:c0 + cg]
        x1_ref[:, c0:c0 + cg] = xin_ref[:, c0:c0 + cg] + gate[:, c0:c0 + cg] * yg

    x1 = x1_ref[...]
    h2 = _norm_mod(x1, nw_ref[1:2, :], _mod_row(mod_ref, 3, is_ctx), _mod_row(mod_ref, 4, is_ctx))
    h2_hi = h2.astype(BF16)
    h2_ref[...] = h2_hi
    g_ref[...] = _router_gates(h2, h2_hi, wr_ref)


def _pool_call(x, y, gmod, mod, nw2, pw, pb, ps, wr, *, n_lat):
    t, d = x.shape
    tp = 128
    cg = d // len(POOL_WINDOWS)
    has_y = y is not None
    n_tiles = t // tp
    bpt = tp // HALO
    last_blk = t // HALO - 1
    row = pl.BlockSpec((tp, d), lambda i: (i, 0))
    prev = pl.BlockSpec((HALO, d), lambda i: (jnp.maximum(i * bpt - 1, 0), 0))
    nxt = pl.BlockSpec((HALO, d), lambda i: (jnp.minimum((i + 1) * bpt, last_blk), 0))
    full = lambda a: pl.BlockSpec(a.shape, lambda i: (0,) * a.ndim)
    ins, specs = [x, x, x], [row, prev, nxt]
    if has_y:
        ins += [y, y, y, gmod]
        specs += [row, prev, nxt, full(gmod)]
    ins += [mod, nw2, pw, pb, ps, wr]
    specs += [full(mod), full(nw2), full(pw), full(pb), full(ps), full(wr)]
    return pl.pallas_call(
        functools.partial(_pool_kernel, has_y=has_y, tp=tp, cg=cg, lat_tiles=n_lat // tp, n_tiles=n_tiles),
        grid=(n_tiles,),
        in_specs=specs,
        out_specs=[row, row, pl.BlockSpec((tp, LANES), lambda i: (i, 0))],
        out_shape=[jax.ShapeDtypeStruct((t, d), F32), jax.ShapeDtypeStruct((t, d), BF16),
                   jax.ShapeDtypeStruct((t, LANES), F32)],
        scratch_shapes=[pltpu.VMEM((tp + 2 * HALO, d), F32), pltpu.VMEM((tp, d), F32)],
        compiler_params=_params(("parallel",)),
        name="pool_layer",
    )(*ins)


def _qk_kernel(a_ref, w_ref, c_ref, sa_ref, sb_ref, o_ref, *, d, tn, qscale):
    acc = jnp.dot(a_ref[...], w_ref[...], preferred_element_type=F32)
    f = jnp.where(pl.program_id(1) < d // tn, qscale, 1.0)
    cc = c_ref[...] * f
    sa = sa_ref[...] * f
    sb = sb_ref[...] * f
    for b in range(tn // LANES):
        blk = acc[:, b * LANES:(b + 1) * LANES]
        rot = blk * cc + pltpu.roll(blk, 96, 1) * sa + pltpu.roll(blk, 32, 1) * sb
        o_ref[:, b * LANES:(b + 1) * LANES] = rot.astype(BF16)


def _vt_kernel(a_ref, w_ref, o_ref, acc_ref):
    acc_ref[...] = jnp.dot(a_ref[...], w_ref[...], preferred_element_type=F32)
    o_ref[...] = acc_ref[...].T.astype(BF16)


def _qkv_call(h, w, rope_c, rope_sa, rope_sb):
    t, d = h.shape
    tm = _pick(t, (640, 512, 256, 128))
    tn = _pick(d, (1024, 512, 256))
    qscale = (DIFF_HEAD_DIM ** -0.5) * LOG2E
    tab = pl.BlockSpec((tm, LANES), lambda i, j: (i, 0))
    a_spec = pl.BlockSpec((tm, d), lambda i, j: (i, 0))
    qk = pl.pallas_call(
        functools.partial(_qk_kernel, d=d, tn=tn, qscale=qscale),
        grid=(t // tm, (2 * d) // tn),
        in_specs=[a_spec, pl.BlockSpec((d, tn), lambda i, j: (0, j)), tab, tab, tab],
        out_specs=pl.BlockSpec((tm, tn), lambda i, j: (i, j)),
        out_shape=jax.ShapeDtypeStruct((t, 2 * d), BF16),
        compiler_params=_params(("parallel", "arbitrary")),
        name="qk_proj",
    )(h, w, rope_c, rope_sa, rope_sb)
    v_col0 = (2 * d) // tn
    vt = pl.pallas_call(
        _vt_kernel,
        grid=(t // tm, d // tn),
        in_specs=[a_spec, pl.BlockSpec((d, tn), lambda i, j: (0, v_col0 + j))],
        out_specs=pl.BlockSpec((tn, tm), lambda i, j: (j, i)),
        out_shape=jax.ShapeDtypeStruct((d, t), BF16),
        scratch_shapes=[pltpu.VMEM((tm, tn), F32)],
        compiler_params=_params(("parallel", "arbitrary")),
        name="v_proj_t",
    )(h, w)
    return qk, vt


def _oproj_kernel(a_ref, w_ref, x_ref, mod_ref, o_ref, *, tm, n_lat):
    acc = jnp.dot(a_ref[...], w_ref[...], preferred_element_type=F32)
    rows = pl.program_id(0) * tm + lax.broadcasted_iota(jnp.int32, (tm, 1), 0)
    gate = jnp.where(rows >= n_lat, mod_ref[MOD_CTX_ROW + 2:MOD_CTX_ROW + 3, :], mod_ref[2:3, :])
    o_ref[...] = x_ref[...] + gate * acc


def _oproj_call(o, w, x, mod, *, n_lat):
    t, d = o.shape
    tm = _pick(t, (640, 512, 256, 128))
    tn = _pick(d, (1024, 512, 256))
    return pl.pallas_call(
        functools.partial(_oproj_kernel, tm=tm, n_lat=n_lat),
        grid=(t // tm, d // tn),
        in_specs=[pl.BlockSpec((tm, d), lambda i, j: (i, 0)), pl.BlockSpec((d, tn), lambda i, j: (0, j)),
                  pl.BlockSpec((tm, tn), lambda i, j: (i, j)), pl.BlockSpec((16, tn), lambda i, j: (0, j))],
        out_specs=pl.BlockSpec((tm, tn), lambda i, j: (i, j)),
        out_shape=jax.ShapeDtypeStruct((t, d), F32),
        compiler_params=_params(("parallel", "arbitrary")),
        name="attn_out_proj",
    )(o, w, x, mod)


def _diff_lambda(lam_ref, lambda_init):
    lv = lam_ref[...]
    a = jnp.sum(lv[0:1, :] * lv[1:2, :], axis=1, keepdims=True)
    b = jnp.sum(lv[2:3, :] * lv[3:4, :], axis=1, keepdims=True)
    return jnp.exp(a) - jnp.exp(b) + lambda_init


def _attn_scores_t(q_ref, k_ref, c):
    qc = q_ref[:, c * DIFF_HEAD_DIM:(c + 1) * DIFF_HEAD_DIM]
    kc = k_ref[:, c * DIFF_HEAD_DIM:(c + 1) * DIFF_HEAD_DIM]
    return lax.dot_general(kc, qc, (((1,), (1,)), ((), ())), preferred_element_type=F32)


def _attn_finish(ot, sw_ref, lambda_init):
    r = lax.rsqrt(jnp.mean(ot * ot, axis=0, keepdims=True) + RMS_EPS)
    return (((ot * r).T * sw_ref[...]) * (1.0 - lambda_init)).astype(BF16)


def _attn_kernel(q_ref, k_ref, vt_ref, lam_ref, sw_ref, o_ref, m_ref, l_ref, acc_ref, *, lambda_init, nk, qp):
    ki = pl.program_id(2)

    @pl.when(ki == 0)
    def _():
        m_ref[...] = jnp.full(m_ref.shape, NEG_BIG, F32)
        l_ref[...] = jnp.zeros_like(l_ref)
        acc_ref[...] = jnp.zeros_like(acc_ref)

    tq = q_ref.shape[0]
    chains = [(c, j) for j in range(tq // qp) for c in range(2)]

    def scores(c, j):
        qc = q_ref[j * qp:(j + 1) * qp, c * DIFF_HEAD_DIM:(c + 1) * DIFF_HEAD_DIM]
        kc = k_ref[:, c * DIFF_HEAD_DIM:(c + 1) * DIFF_HEAD_DIM]
        return lax.dot_general(kc, qc, (((1,), (1,)), ((), ())), preferred_element_type=F32)

    s_next = scores(*chains[0])
    for n, (c, j) in enumerate(chains):
        s = s_next
        if n + 1 < len(chains):
            s_next = scores(*chains[n + 1])
        cols = slice(j * qp, (j + 1) * qp)
        m_prev = m_ref[c, :, cols]
        m_new = jnp.maximum(m_prev, jnp.max(s, axis=0, keepdims=True))
        alpha = jnp.exp2(m_prev - m_new)
        p = jnp.exp2(s - m_new)
        l_ref[c, :, cols] = alpha * l_ref[c, :, cols] + jnp.sum(p, axis=0, keepdims=True)
        acc_ref[c, :, cols] = alpha * acc_ref[c, :, cols] + jnp.dot(
            vt_ref[...], p.astype(BF16), preferred_element_type=F32)
        m_ref[c, :, cols] = m_new

    @pl.when(ki == nk - 1)
    def _():
        lam = _diff_lambda(lam_ref, lambda_init)
        ot = acc_ref[0] / l_ref[0] - lam * (acc_ref[1] / l_ref[1])
        o_ref[...] = _attn_finish(ot, sw_ref, lambda_init)


def _attn_ctx_kernel(q_ref, k_ref, vt_ref, lam_ref, sw_ref, oin_ref, o_ref, *, lambda_init):
    del oin_ref
    lam = _diff_lambda(lam_ref, lambda_init)
    outs = []
    for c in range(2):
        s = _attn_scores_t(q_ref, k_ref, c)
        p = jnp.exp2(s - jnp.max(s, axis=0, keepdims=True))
        l = jnp.sum(p, axis=0, keepdims=True)
        outs.append(jnp.dot(vt_ref[...], p.astype(BF16), preferred_element_type=F32) / l)
    o_ref[...] = _attn_finish(outs[0] - lam * outs[1], sw_ref, lambda_init)


def _attn_call(qk, vt, lam_vecs, subln_w, *, lambda_init, n_lat, n_ctx):
    d, t = vt.shape
    heads = d // HEAD_W
    tq = _pick(n_lat, (1024, 512, 256, 128))
    tk = _pick(t, (1280, 640, 256, 128))
    nk = t // tk
    cblk = n_lat // n_ctx
    sw = subln_w.reshape(1, HEAD_W)
    small = lambda a: pl.BlockSpec(a.shape, lambda *_: (0,) * a.ndim)
    o_lat = pl.pallas_call(
        functools.partial(_attn_kernel, lambda_init=lambda_init, nk=nk, qp=min(tq, 256)),
        grid=(heads, n_lat // tq, nk),
        in_specs=[
            pl.BlockSpec((tq, HEAD_W), lambda h, i, k: (i, h)),
            pl.BlockSpec((tk, HEAD_W), lambda h, i, k: (k, heads + h)),
            pl.BlockSpec((HEAD_W, tk), lambda h, i, k: (h, k)),
            small(lam_vecs), small(sw),
        ],
        out_specs=pl.BlockSpec((tq, HEAD_W), lambda h, i, k: (i, h)),
        out_shape=jax.ShapeDtypeStruct((t, d), BF16),
        scratch_shapes=[pltpu.VMEM((2, 1, tq), F32), pltpu.VMEM((2, 1, tq), F32),
                        pltpu.VMEM((2, HEAD_W, tq), F32)],
        compiler_params=_params(("parallel", "parallel", "arbitrary")),
        name="diff_attn",
    )(qk, qk, vt, lam_vecs, sw)
    return pl.pallas_call(
        functools.partial(_attn_ctx_kernel, lambda_init=lambda_init),
        grid=(heads,),
        in_specs=[
            pl.BlockSpec((n_ctx, HEAD_W), lambda h: (cblk, h)),
            pl.BlockSpec((n_ctx, HEAD_W), lambda h: (cblk, heads + h)),
            pl.BlockSpec((HEAD_W, n_ctx), lambda h: (h, cblk)),
            small(lam_vecs), small(sw),
            pl.BlockSpec(memory_space=pl.ANY),
        ],
        out_specs=pl.BlockSpec((n_ctx, HEAD_W), lambda h: (cblk, h)),
        out_shape=jax.ShapeDtypeStruct((t, d), BF16),
        input_output_aliases={5: 0},
        compiler_params=_params(("arbitrary",)),
        name="diff_attn_ctx",
    )(qk, qk, vt, lam_vecs, sw, o_lat)


def _moe_kernel(h_ref, g_ref, w1_ref, w3_ref, w2_ref, y_ref):
    e = pl.program_id(1)

    @pl.when(e == 0)
    def _():
        y_ref[...] = jnp.zeros_like(y_ref)

    hv = h_ref[...]
    a = jnp.dot(hv, w1_ref[...], preferred_element_type=F32)
    u = jnp.dot(hv, w3_ref[...], preferred_element_type=F32)
    gt = g_ref[...]
    col = lax.broadcasted_iota(jnp.int32, gt.shape, 1)
    ge = jnp.sum(jnp.where(col == e + ROUTER_COL0, gt, 0.0), axis=1, keepdims=True)
    hid = (_silu(a) * u) * ge
    y_ref[...] += jnp.dot(hid.astype(BF16), w2_ref[...], preferred_element_type=F32)


def _moe_call(h, gates, w1, w3, w2):
    t, d = h.shape
    n_e, _, f = w1.shape
    tm = _pick(t, (640, 512, 256, 128))
    return pl.pallas_call(
        _moe_kernel,
        grid=(t // tm, n_e),
        in_specs=[
            pl.BlockSpec((tm, d), lambda i, e: (i, 0)),
            pl.BlockSpec((tm, LANES), lambda i, e: (i, 0)),
            pl.BlockSpec((None, d, f), lambda i, e: (e, 0, 0)),
            pl.BlockSpec((None, d, f), lambda i, e: (e, 0, 0)),
            pl.BlockSpec((None, f, d), lambda i, e: (e, 0, 0)),
        ],
        out_specs=pl.BlockSpec((tm, d), lambda i, e: (i, 0)),
        out_shape=jax.ShapeDtypeStruct((t, d), F32),
        compiler_params=_params(("parallel", "arbitrary")),
        name="moe_dense",
    )(h, gates, w1, w3, w2)


def _final_kernel(x_ref, y_ref, gmod_ref, fw_ref, o_ref):
    xv = x_ref[...] + gmod_ref[5:6, :] * y_ref[...]
    r = lax.rsqrt(jnp.mean(xv * xv, axis=-1, keepdims=True) + RMS_EPS)
    o_ref[...] = (xv * r) * fw_ref[...]


def _final_call(x, y, gmod, fw, *, n_lat):
    _, d = x.shape
    tm = 256
    row = pl.BlockSpec((tm, d), lambda i: (i, 0))
    full = lambda a: pl.BlockSpec(a.shape, lambda i: (0,) * a.ndim)
    return pl.pallas_call(
        _final_kernel,
        grid=(n_lat // tm,),
        in_specs=[row, row, full(gmod), full(fw)],
        out_specs=row,
        out_shape=jax.ShapeDtypeStruct((n_lat, d), F32),
        compiler_params=_params(("parallel",)),
        name="final_norm",
    )(x, y, gmod, fw)


def _rope_tables(n_lat, n_ctx):
    axis_dim = DIFF_HEAD_DIM // 2
    quarter = axis_dim // 2
    rows = n_lat // GRID_W
    row = jnp.broadcast_to(jnp.arange(rows, dtype=F32)[:, None], (rows, GRID_W)).reshape(-1)
    col = jnp.broadcast_to(jnp.arange(GRID_W, dtype=F32)[None, :], (rows, GRID_W)).reshape(-1)
    inv_freq = ROPE_THETA ** (-jnp.arange(0, axis_dim, 2, dtype=F32) / axis_dim)
    ang_r = row[:, None] * inv_freq
    ang_c = col[:, None] * inv_freq
    zeros = jnp.zeros((n_lat, quarter), F32)
    c = jnp.concatenate([jnp.cos(ang_r), jnp.cos(ang_r), jnp.cos(ang_c), jnp.cos(ang_c)], axis=1)
    sa = jnp.concatenate([-jnp.sin(ang_r), zeros, -jnp.sin(ang_c), zeros], axis=1)
    sb = jnp.concatenate([zeros, jnp.sin(ang_r), zeros, jnp.sin(ang_c)], axis=1)
    pad = lambda a, v: jnp.concatenate([a, jnp.full((n_ctx, DIFF_HEAD_DIM), v, F32)], axis=0)
    return pad(c, 1.0), pad(sa, 0.0), pad(sb, 0.0)


def _pack_mod(m, d):
    z = jnp.zeros((MOD_CTX_ROW - N_MOD, d), F32)
    return jnp.concatenate([m[0].reshape(N_MOD, d), z, m[1].reshape(N_MOD, d), z], axis=0)


def kernel(x, c, ctx, c_ctx, ada_w, ada_b, norm_w, pool_w, pool_b, pool_scale, attn_w_qkv, attn_w_o,
           attn_lambda, attn_subln_w, router_group_w, router_expert_w, expert_w1, expert_w3, expert_w2,
           final_norm_w):
    _, n_lat, d = x.shape
    n_ctx = ctx.shape[1]
    depth = ada_w.shape[0]
    assert x.shape[0] == 1 and n_lat % n_ctx == 0 and n_ctx % 256 == 0 and d % (4 * LANES) == 0

    xs = jnp.concatenate([x[0], ctx[0]], axis=0)
    mods = _adaln(c, c_ctx, ada_w, ada_b)
    mod = [_pack_mod(mods[i], d) for i in range(depth)]

    wr = jnp.concatenate([router_group_w, router_expert_w], axis=-1)
    wr = jnp.pad(wr, ((0, 0), (0, 0), (0, LANES - wr.shape[-1])))
    wr_hi = wr.astype(BF16)
    wr_lo = (wr - wr_hi.astype(F32)).astype(BF16)
    wr2 = jnp.concatenate([wr_hi, wr_lo], axis=-1)

    pool_w16 = pool_w.astype(BF16)
    wqkv16 = attn_w_qkv.astype(BF16)
    wo16 = attn_w_o.astype(BF16)
    w1_16, w3_16, w2_16 = expert_w1.astype(BF16), expert_w3.astype(BF16), expert_w2.astype(BF16)
    rope_c, rope_sa, rope_sb = _rope_tables(n_lat, n_ctx)

    y = None
    for i in range(depth):
        jm = i // 2
        gprev = mod[i - 1] if i > 0 else None
        if i % 2 == 0:
            xs, h2, gates = _pool_call(xs, y, gprev, mod[i], norm_w[i], pool_w16[jm], pool_b[jm],
                                       pool_scale[jm].reshape(1, d), wr2[i], n_lat=n_lat)
        else:
            lambda_init = 0.8 - 0.6 * math.exp(-0.3 * i)
            xs, h, _ = _norm_call(xs, y, gprev, mod[i], norm_w[i, 0:1], None, k_shift=0, n_lat=n_lat)
            qk, vt = _qkv_call(h, wqkv16[jm], rope_c, rope_sa, rope_sb)
            o = _attn_call(qk, vt, attn_lambda[jm], attn_subln_w[jm], lambda_init=lambda_init,
                           n_lat=n_lat, n_ctx=n_ctx)
            xs = _oproj_call(o, wo16[jm], xs, mod[i], n_lat=n_lat)
            _, h2, gates = _norm_call(xs, None, None, mod[i], norm_w[i, 1:2], wr2[i], k_shift=3, n_lat=n_lat)
        y = _moe_call(h2, gates, w1_16[i], w3_16[i], w2_16[i])
    out = _final_call(xs, y, mod[depth - 1], final_norm_w.reshape(1, d), n_lat=n_lat)
    return out[None]
```

```python
import functools
import math

import jax
import jax.numpy as jnp
from jax import lax
from jax.experimental import pallas as pl
from jax.experimental.pallas import tpu as pltpu

F32 = jnp.float32
BF16 = jnp.bfloat16

GRID_W = 64
RMS_EPS = 1e-6
ROPE_THETA = 10000.0
POOL_WINDOWS = (2, 4, 8, 16)
N_MOD = 6
N_EXPERT_GROUPS = 4
EXPERTS_PER_GROUP = 4
N_EXPERTS = N_EXPERT_GROUPS * EXPERTS_PER_GROUP
DIFF_HEAD_DIM = 128
HEAD_W = 2 * DIFF_HEAD_DIM
LANES = 128
SUBLANES = 8
MOD_CTX_ROW = 8
ROUTER_COL0 = N_EXPERT_GROUPS
PAIRS_PER_GROUP = 6
MOE_TILE = 128
ADALN_COLS = 2048
VMEM_LIMIT = 56 * 1024 * 1024
NEG_BIG = -1e30
LOG2E = 1.4426950408889634


def _pick(n, candidates):
    for c in candidates:
        if c <= n and n % c == 0:
            return c
    return n


def _params(semantics):
    return pltpu.CompilerParams(dimension_semantics=semantics, vmem_limit_bytes=VMEM_LIMIT)


def _silu(v):
    return v * jax.nn.sigmoid(v)


def _adaln_kernel(c_ref, cc_ref, w_ref, b_ref, o_ref, acc_ref, s_ref, *, tk, tn, nk):
    k = pl.program_id(2)

    @pl.when(k == 0)
    def _():
        acc_ref[...] = jnp.zeros_like(acc_ref)

    s_ref[0] = _silu(c_ref[...])
    s_ref[1] = _silu(cc_ref[...])

    cw = min(tn, ADALN_COLS)
    reps = cw // LANES
    for c0 in range(0, tn, cw):
        def body(j, carry, c0=c0):
            a0, a1 = carry
            r = pl.multiple_of(j * SUBLANES, SUBLANES)
            w = w_ref[pl.ds(r, SUBLANES), c0:c0 + cw]
            a0 = a0 + w * jnp.concatenate([s_ref[0, pl.ds(r, SUBLANES), :]] * reps, axis=1)
            a1 = a1 + w * jnp.concatenate([s_ref[1, pl.ds(r, SUBLANES), :]] * reps, axis=1)
            return a0, a1

        z = jnp.zeros((SUBLANES, cw), F32)
        a0, a1 = lax.fori_loop(0, tk // SUBLANES, body, (z, z), unroll=4)
        acc_ref[0, :, c0:c0 + cw] += a0
        acc_ref[1, :, c0:c0 + cw] += a1

    @pl.when(k == nk - 1)
    def _():
        o_ref[0:1, :] = jnp.sum(acc_ref[0], axis=0, keepdims=True) + b_ref[...]
        o_ref[1:2, :] = jnp.sum(acc_ref[1], axis=0, keepdims=True) + b_ref[...]


def _adaln(c, c_ctx, ada_w, ada_b):
    depth, d, n = ada_w.shape
    tk = _pick(d, (512, 256, 128))
    tn = _pick(n, (4096, 2048, 1024, 512, 256, 128))
    nk = d // tk
    c_b = jnp.broadcast_to(c.reshape(d, 1), (d, LANES))
    cc_b = jnp.broadcast_to(c_ctx.reshape(d, 1), (d, LANES))
    return pl.pallas_call(
        functools.partial(_adaln_kernel, tk=tk, tn=tn, nk=nk),
        grid=(depth, n // tn, nk),
        in_specs=[
            pl.BlockSpec((tk, LANES), lambda l, j, k: (k, 0)),
            pl.BlockSpec((tk, LANES), lambda l, j, k: (k, 0)),
            pl.BlockSpec((None, tk, tn), lambda l, j, k: (l, k, j)),
            pl.BlockSpec((None, 1, tn), lambda l, j, k: (l, 0, j)),
        ],
        out_specs=pl.BlockSpec((None, 2, tn), lambda l, j, k: (l, 0, j)),
        out_shape=jax.ShapeDtypeStruct((depth, 2, n), F32),
        scratch_shapes=[pltpu.VMEM((2, SUBLANES, tn), F32), pltpu.VMEM((2, tk, LANES), F32)],
        compiler_params=_params(("parallel", "parallel", "arbitrary")),
        name="adaln",
    )(c_b, cc_b, ada_w, ada_b.reshape(depth, 1, n))


def _mod_row(mod_ref, k, is_ctx):
    return jnp.where(is_ctx, mod_ref[MOD_CTX_ROW + k:MOD_CTX_ROW + k + 1, :], mod_ref[k:k + 1, :])


def _norm_mod(xv, w_row, shift, scale):
    r = lax.rsqrt(jnp.mean(xv * xv, axis=-1, keepdims=True) + RMS_EPS)
    return (xv * r) * (w_row * (1.0 + scale)) + shift


def _router_info(h, h_hi, wr_ref):
    h_lo = (h - h_hi.astype(F32)).astype(BF16)
    r1 = jnp.dot(h_hi, wr_ref[...], preferred_element_type=F32)
    r2 = jnp.dot(h_lo, wr_ref[:, :LANES], preferred_element_type=F32)
    logits = r1[:, :LANES] + r1[:, LANES:] + r2
    col = lax.broadcasted_iota(jnp.int32, logits.shape, 1).astype(F32)
    far = float(4 * LANES)
    is_g = col < float(N_EXPERT_GROUPS)
    mg = jnp.max(jnp.where(is_g, logits, NEG_BIG), axis=1, keepdims=True)
    denom = jnp.sum(jnp.where(is_g, jnp.exp(logits - mg), 0.0), axis=1, keepdims=True)
    pg_top = 1.0 / denom
    g_sel = jnp.min(jnp.where(is_g & (logits == mg), col, far), axis=1, keepdims=True)
    e_lo = float(ROUTER_COL0) + float(EXPERTS_PER_GROUP) * g_sel
    in_grp = (col >= e_lo) & (col < e_lo + float(EXPERTS_PER_GROUP))
    v1 = jnp.max(jnp.where(in_grp, logits, NEG_BIG), axis=1, keepdims=True)
    i1 = jnp.min(jnp.where(in_grp & (logits == v1), col, far), axis=1, keepdims=True)
    rest = in_grp & (col != i1)
    v2 = jnp.max(jnp.where(rest, logits, NEG_BIG), axis=1, keepdims=True)
    i2 = jnp.min(jnp.where(rest & (logits == v2), col, far), axis=1, keepdims=True)
    e2 = jnp.exp(v2 - v1)
    w1 = pg_top / (1.0 + e2)
    w2 = pg_top * (e2 / (1.0 + e2))
    first_lower = i1 < i2
    ja = jnp.minimum(i1, i2) - e_lo
    jb = jnp.maximum(i1, i2) - e_lo
    bucket = g_sel * float(PAIRS_PER_GROUP) + ja * (7.0 - ja) * 0.5 + (jb - ja - 1.0)
    g_lower = jnp.where(first_lower, w1, w2)
    g_higher = jnp.where(first_lower, w2, w1)
    return jnp.where(col == 0.0, bucket, jnp.where(col == 1.0, g_lower, jnp.where(col == 2.0, g_higher, 0.0)))


def _emit_routed_rows(h, wr_ref, hx_ref, info_ref):
    d = h.shape[1]
    info = _router_info(h, h.astype(BF16), wr_ref)
    hx_ref[:, :d] = h
    hx_ref[:, d:] = info
    info_ref[...] = info


def _norm_kernel(*refs, has_y, router, k_shift, lat_tiles):
    it = iter(refs)
    x_ref = next(it)
    y_ref = next(it) if has_y else None
    gmod_ref = next(it) if has_y else None
    mod_ref = next(it)
    nw_ref = next(it)
    wr_ref = next(it) if router else None
    xo_ref = next(it) if has_y else None
    h_ref = next(it)
    g_ref = next(it) if router else None

    is_ctx = pl.program_id(0) >= lat_tiles
    xv = x_ref[...]
    if has_y:
        xv = xv + _mod_row(gmod_ref, 5, is_ctx) * y_ref[...]
        xo_ref[...] = xv
    h = _norm_mod(xv, nw_ref[...], _mod_row(mod_ref, k_shift, is_ctx), _mod_row(mod_ref, k_shift + 1, is_ctx))
    if router:
        _emit_routed_rows(h, wr_ref, h_ref, g_ref)
    else:
        h_ref[...] = h.astype(BF16)


def _norm_call(x, y, gmod, mod, nw, wr, *, k_shift, n_lat):
    t, d = x.shape
    tm = 256
    has_y = y is not None
    router = wr is not None
    row = pl.BlockSpec((tm, d), lambda i: (i, 0))
    full = lambda a: pl.BlockSpec(a.shape, lambda i: (0,) * a.ndim)
    ins, specs = [x], [row]
    if has_y:
        ins += [y, gmod]
        specs += [row, full(gmod)]
    ins += [mod, nw]
    specs += [full(mod), full(nw)]
    if router:
        ins.append(wr)
        specs.append(full(wr))
    outs, ospecs = [], []
    if has_y:
        outs.append(jax.ShapeDtypeStruct((t, d), F32))
        ospecs.append(row)
    if router:
        outs += [jax.ShapeDtypeStruct((t, d + LANES), F32), jax.ShapeDtypeStruct((t, LANES), F32)]
        ospecs += [pl.BlockSpec((tm, d + LANES), lambda i: (i, 0)), pl.BlockSpec((tm, LANES), lambda i: (i, 0))]
    else:
        outs.append(jax.ShapeDtypeStruct((t, d), BF16))
        ospecs.append(row)
    res = pl.pallas_call(
        functools.partial(_norm_kernel, has_y=has_y, router=router, k_shift=k_shift, lat_tiles=n_lat // tm),
        grid=(t // tm,),
        in_specs=specs,
        out_specs=ospecs,
        out_shape=outs,
        compiler_params=_params(("parallel",)),
        name="norm_mod",
    )(*ins)
    res = list(res)
    xo = res.pop(0) if has_y else x
    h = res.pop(0)
    g = res.pop(0) if router else None
    return xo, h, g


HALO = 8


def _pool_kernel(*refs, has_y, tp, cg, lat_tiles, n_tiles):
    it = iter(refs)
    x_ref, xp_ref, xn_ref = next(it), next(it), next(it)
    if has_y:
        y_ref, yp_ref, yn_ref, gmod_ref = next(it), next(it), next(it), next(it)
    mod_ref, nw_ref, pw_ref, pb_ref, ps_ref, wr_ref = (next(it) for _ in range(6))
    x1_ref, h2_ref, g_ref = next(it), next(it), next(it)
    hext_ref, xin_ref = next(it), next(it)

    j = pl.program_id(0)
    is_ctx = j >= lat_tiles
    first = (j == 0) | (j == lat_tiles)
    last = (j == lat_tiles - 1) | (j == n_tiles - 1)

    nw = nw_ref[0:1, :]
    ws = nw * (1.0 + _mod_row(mod_ref, 1, is_ctx))
    shift = _mod_row(mod_ref, 0, is_ctx)

    def x_in(xr, yr):
        xv = xr[...]
        if has_y:
            xv = xv + _mod_row(gmod_ref, 5, is_ctx) * yr[...]
        return xv

    def h_of(xv):
        r = lax.rsqrt(jnp.mean(xv * xv, axis=-1, keepdims=True) + RMS_EPS)
        return (xv * r) * ws + shift

    xm = x_in(x_ref, y_ref if has_y else None)
    xin_ref[...] = xm
    hext_ref[HALO:HALO + tp, :] = h_of(xm)
    hp = h_of(x_in(xp_ref, yp_ref if has_y else None))
    hext_ref[0:HALO, :] = jnp.where(first, 0.0, hp)
    hn = h_of(x_in(xn_ref, yn_ref if has_y else None))
    hext_ref[HALO + tp:HALO + tp + HALO, :] = jnp.where(last, 0.0, hn)

    tpos = lax.broadcasted_iota(jnp.int32, (tp, 1), 0).astype(F32)
    lo_lim = jnp.where(first, 0.0, -float(2 * HALO))
    hi_lim = jnp.where(last, float(tp), float(tp + 2 * HALO))
    gate = _mod_row(mod_ref, 2, is_ctx)
    for g, w in enumerate(POOL_WINDOWS):
        c0 = g * cg
        half = w // 2
        wsum = hext_ref[pl.ds(HALO - half, tp), c0:c0 + cg]
        for dd in range(-half + 1, half):
            wsum = wsum + hext_ref[pl.ds(HALO + dd, tp), c0:c0 + cg]
        cnt = jnp.minimum(tpos + float(half), hi_lim) - jnp.maximum(tpos - float(half), lo_lim)
        pooled = wsum / cnt - hext_ref[HALO:HALO + tp, c0:c0 + cg]
        yg = jnp.dot(pooled.astype(BF16), pw_ref[g], preferred_element_type=F32) + pb_ref[g:g + 1, :]
        yg = yg * ps_ref[:, c0:c0 + cg]
        x1_ref[:, c0:c0 + cg] = xin_ref[:, c0:c0 + cg] + gate[:, c0:c0 + cg] * yg

    x1 = x1_ref[...]
    h2 = _norm_mod(x1, nw_ref[1:2, :], _mod_row(mod_ref, 3, is_ctx), _mod_row(mod_ref, 4, is_ctx))
    _emit_routed_rows(h2, wr_ref, h2_ref, g_ref)


def _pool_call(x, y, gmod, mod, nw2, pw, pb, ps, wr, *, n_lat):
    t, d = x.shape
    tp = 128
    cg = d // len(POOL_WINDOWS)
    has_y = y is not None
    n_tiles = t // tp
    bpt = tp // HALO
    last_blk = t // HALO - 1
    row = pl.BlockSpec((tp, d), lambda i: (i, 0))
    prev = pl.BlockSpec((HALO, d), lambda i: (jnp.maximum(i * bpt - 1, 0), 0))
    nxt = pl.BlockSpec((HALO, d), lambda i: (jnp.minimum((i + 1) * bpt, last_blk), 0))
    full = lambda a: pl.BlockSpec(a.shape, lambda i: (0,) * a.ndim)
    ins, specs = [x, x, x], [row, prev, nxt]
    if has_y:
        ins += [y, y, y, gmod]
        specs += [row, prev, nxt, full(gmod)]
    ins += [mod, nw2, pw, pb, ps, wr]
    specs += [full(mod), full(nw2), full(pw), full(pb), full(ps), full(wr)]
    return pl.pallas_call(
        functools.partial(_pool_kernel, has_y=has_y, tp=tp, cg=cg, lat_tiles=n_lat // tp, n_tiles=n_tiles),
        grid=(n_tiles,),
        in_specs=specs,
        out_specs=[row, pl.BlockSpec((tp, d + LANES), lambda i: (i, 0)), pl.BlockSpec((tp, LANES), lambda i: (i, 0))],
        out_shape=[jax.ShapeDtypeStruct((t, d), F32), jax.ShapeDtypeStruct((t, d + LANES), F32),
                   jax.ShapeDtypeStruct((t, LANES), F32)],
        scratch_shapes=[pltpu.VMEM((tp + 2 * HALO, d), F32), pltpu.VMEM((tp, d), F32)],
        compiler_params=_params(("parallel",)),
        name="pool_layer",
    )(*ins)


def _qk_kernel(a_ref, w_ref, c_ref, sa_ref, sb_ref, o_ref, *, d, tn, qscale):
    acc = jnp.dot(a_ref[...], w_ref[...], preferred_element_type=F32)
    f = jnp.where(pl.program_id(1) < d // tn, qscale, 1.0)
    cc = c_ref[...] * f
    sa = sa_ref[...] * f
    sb = sb_ref[...] * f
    for b in range(tn // LANES):
        blk = acc[:, b * LANES:(b + 1) * LANES]
        rot = blk * cc + pltpu.roll(blk, 96, 1) * sa + pltpu.roll(blk, 32, 1) * sb
        o_ref[:, b * LANES:(b + 1) * LANES] = rot.astype(BF16)


def _vt_kernel(a_ref, w_ref, o_ref, acc_ref):
    acc_ref[...] = jnp.dot(a_ref[...], w_ref[...], preferred_element_type=F32)
    o_ref[...] = acc_ref[...].T.astype(BF16)


def _qkv_call(h, w, rope_c, rope_sa, rope_sb):
    t, d = h.shape
    tm = _pick(t, (640, 512, 256, 128))
    tn = _pick(d, (1024, 512, 256))
    qscale = (DIFF_HEAD_DIM ** -0.5) * LOG2E
    tab = pl.BlockSpec((tm, LANES), lambda i, j: (i, 0))
    a_spec = pl.BlockSpec((tm, d), lambda i, j: (i, 0))
    qk = pl.pallas_call(
        functools.partial(_qk_kernel, d=d, tn=tn, qscale=qscale),
        grid=(t // tm, (2 * d) // tn),
        in_specs=[a_spec, pl.BlockSpec((d, tn), lambda i, j: (0, j)), tab, tab, tab],
        out_specs=pl.BlockSpec((tm, tn), lambda i, j: (i, j)),
        out_shape=jax.ShapeDtypeStruct((t, 2 * d), BF16),
        compiler_params=_params(("parallel", "arbitrary")),
        name="qk_proj",
    )(h, w, rope_c, rope_sa, rope_sb)
    v_col0 = (2 * d) // tn
    vt = pl.pallas_call(
        _vt_kernel,
        grid=(t // tm, d // tn),
        in_specs=[a_spec, pl.BlockSpec((d, tn), lambda i, j: (0, v_col0 + j))],
        out_specs=pl.BlockSpec((tn, tm), lambda i, j: (j, i)),
        out_shape=jax.ShapeDtypeStruct((d, t), BF16),
        scratch_shapes=[pltpu.VMEM((tm, tn), F32)],
        compiler_params=_params(("parallel", "arbitrary")),
        name="v_proj_t",
    )(h, w)
    return qk, vt


def _oproj_kernel(a_ref, w_ref, x_ref, mod_ref, o_ref, *, tm, n_lat):
    acc = jnp.dot(a_ref[...], w_ref[...], preferred_element_type=F32)
    rows = pl.program_id(0) * tm + lax.broadcasted_iota(jnp.int32, (tm, 1), 0)
    gate = jnp.where(rows >= n_lat, mod_ref[MOD_CTX_ROW + 2:MOD_CTX_ROW + 3, :], mod_ref[2:3, :])
    o_ref[...] = x_ref[...] + gate * acc


def _oproj_call(o, w, x, mod, *, n_lat):
    t, d = x.shape
    tm = _pick(t, (640, 512, 256, 128))
    tn = _pick(d, (1024, 512, 256))
    return pl.pallas_call(
        functools.partial(_oproj_kernel, tm=tm, n_lat=n_lat),
        grid=(t // tm, d // tn),
        in_specs=[pl.BlockSpec((tm, d), lambda i, j: (i, 0)), pl.BlockSpec((d, tn), lambda i, j: (0, j)),
                  pl.BlockSpec((tm, tn), lambda i, j: (i, j)), pl.BlockSpec((16, tn), lambda i, j: (0, j))],
        out_specs=pl.BlockSpec((tm, tn), lambda i, j: (i, j)),
        out_shape=jax.ShapeDtypeStruct((t, d), F32),
        compiler_params=_params(("parallel", "arbitrary")),
        name="attn_out_proj",
    )(o, w, x, mod)


def _diff_lambda(lam_ref, lambda_init):
    lv = lam_ref[...]
    a = jnp.sum(lv[0:1, :] * lv[1:2, :], axis=1, keepdims=True)
    b = jnp.sum(lv[2:3, :] * lv[3:4, :], axis=1, keepdims=True)
    return jnp.exp(a) - jnp.exp(b) + lambda_init


def _attn_scores_t(q_ref, k_ref, c):
    qc = q_ref[:, c * DIFF_HEAD_DIM:(c + 1) * DIFF_HEAD_DIM]
    kc = k_ref[:, c * DIFF_HEAD_DIM:(c + 1) * DIFF_HEAD_DIM]
    return lax.dot_general(kc, qc, (((1,), (1,)), ((), ())), preferred_element_type=F32)


def _attn_finish(ot, sw_ref, lambda_init):
    r = lax.rsqrt(jnp.mean(ot * ot, axis=0, keepdims=True) + RMS_EPS)
    return (((ot * r).T * sw_ref[...]) * (1.0 - lambda_init)).astype(BF16)


def _attn_kernel(q_ref, k_ref, vt_ref, octx_ref, lam_ref, sw_ref, o_ref, m_ref, l_ref, acc_ref,
                 *, lambda_init, nq, nk, qp):
    qi = pl.program_id(1)
    ki = pl.program_id(2)

    @pl.when(qi < nq)
    def _():
        _attn_step(q_ref, k_ref, vt_ref, lam_ref, sw_ref, o_ref, m_ref, l_ref, acc_ref, ki,
                   lambda_init=lambda_init, nk=nk, qp=qp)

    @pl.when((qi == nq) & (ki == nk - 1))
    def _():
        n_ctx = octx_ref.shape[0]
        o_ref[0:n_ctx, :] = octx_ref[...]
        o_ref[n_ctx:, :] = jnp.zeros((o_ref.shape[0] - n_ctx, o_ref.shape[1]), o_ref.dtype)


def _attn_step(q_ref, k_ref, vt_ref, lam_ref, sw_ref, o_ref, m_ref, l_ref, acc_ref, ki, *, lambda_init, nk, qp):
    @pl.when(ki == 0)
    def _():
        m_ref[...] = jnp.full(m_ref.shape, NEG_BIG, F32)
        l_ref[...] = jnp.zeros_like(l_ref)
        acc_ref[...] = jnp.zeros_like(acc_ref)

    tq = q_ref.shape[0]
    chains = [(c, j) for j in range(tq // qp) for c in range(2)]

    def scores(c, j):
        qc = q_ref[j * qp:(j + 1) * qp, c * DIFF_HEAD_DIM:(c + 1) * DIFF_HEAD_DIM]
        kc = k_ref[:, c * DIFF_HEAD_DIM:(c + 1) * DIFF_HEAD_DIM]
        return lax.dot_general(kc, qc, (((1,), (1,)), ((), ())), preferred_element_type=F32)

    s_next = scores(*chains[0])
    for n, (c, j) in enumerate(chains):
        s = s_next
        if n + 1 < len(chains):
            s_next = scores(*chains[n + 1])
        cols = slice(j * qp, (j + 1) * qp)
        m_prev = m_ref[c, :, cols]
        m_new = jnp.maximum(m_prev, jnp.max(s, axis=0, keepdims=True))
        alpha = jnp.exp2(m_prev - m_new)
        p = jnp.exp2(s - m_new)
        l_ref[c, :, cols] = alpha * l_ref[c, :, cols] + jnp.sum(p, axis=0, keepdims=True)
        acc_ref[c, :, cols] = alpha * acc_ref[c, :, cols] + jnp.dot(
            vt_ref[...], p.astype(BF16), preferred_element_type=F32)
        m_ref[c, :, cols] = m_new

    @pl.when(ki == nk - 1)
    def _():
        lam = _diff_lambda(lam_ref, lambda_init)
        ot = acc_ref[0] / l_ref[0] - lam * (acc_ref[1] / l_ref[1])
        o_ref[...] = _attn_finish(ot, sw_ref, lambda_init)


def _attn_ctx_kernel(q_ref, k_ref, vt_ref, lam_ref, sw_ref, o_ref, *, lambda_init):
    lam = _diff_lambda(lam_ref, lambda_init)
    outs = []
    for c in range(2):
        s = _attn_scores_t(q_ref, k_ref, c)
        p = jnp.exp2(s - jnp.max(s, axis=0, keepdims=True))
        l = jnp.sum(p, axis=0, keepdims=True)
        outs.append(jnp.dot(vt_ref[...], p.astype(BF16), preferred_element_type=F32) / l)
    o_ref[...] = _attn_finish(outs[0] - lam * outs[1], sw_ref, lambda_init)


def _attn_call(qk, vt, lam_vecs, subln_w, *, lambda_init, n_lat, n_ctx):
    d, t = vt.shape
    heads = d // HEAD_W
    tq = _pick(n_lat, (1024, 512, 256, 128))
    tk = _pick(t, (3328, 1280, 640, 256, 128))
    nq = n_lat // tq
    nk = t // tk
    cblk = n_lat // n_ctx
    assert n_ctx <= tq
    sw = subln_w.reshape(1, HEAD_W)
    small = lambda a: pl.BlockSpec(a.shape, lambda *_: (0,) * a.ndim)
    o_ctx = pl.pallas_call(
        functools.partial(_attn_ctx_kernel, lambda_init=lambda_init),
        grid=(heads,),
        in_specs=[
            pl.BlockSpec((n_ctx, HEAD_W), lambda h: (cblk, h)),
            pl.BlockSpec((n_ctx, HEAD_W), lambda h: (cblk, heads + h)),
            pl.BlockSpec((HEAD_W, n_ctx), lambda h: (h, cblk)),
            small(lam_vecs), small(sw),
        ],
        out_specs=pl.BlockSpec((n_ctx, HEAD_W), lambda h: (0, h)),
        out_shape=jax.ShapeDtypeStruct((n_ctx, d), BF16),
        compiler_params=_params(("parallel",)),
        name="diff_attn_ctx",
    )(qk, qk, vt, lam_vecs, sw)
    last_q = nq - 1
    return pl.pallas_call(
        functools.partial(_attn_kernel, lambda_init=lambda_init, nq=nq, nk=nk, qp=min(tq, 256)),
        grid=(heads, nq + 1, nk),
        in_specs=[
            pl.BlockSpec((tq, HEAD_W), lambda h, i, k: (jnp.minimum(i, last_q), h)),
            pl.BlockSpec((tk, HEAD_W), lambda h, i, k: (k, heads + h)),
            pl.BlockSpec((HEAD_W, tk), lambda h, i, k: (h, k)),
            pl.BlockSpec((n_ctx, HEAD_W), lambda h, i, k: (0, h)),
            small(lam_vecs), small(sw),
        ],
        out_specs=pl.BlockSpec((tq, HEAD_W), lambda h, i, k: (i, h)),
        out_shape=jax.ShapeDtypeStruct(((nq + 1) * tq, d), BF16),
        scratch_shapes=[pltpu.VMEM((2, 1, tq), F32), pltpu.VMEM((2, 1, tq), F32),
                        pltpu.VMEM((2, HEAD_W, tq), F32)],
        compiler_params=_params(("parallel", "parallel", "arbitrary")),
        name="diff_attn",
    )(qk, qk, vt, o_ctx, lam_vecs, sw)


def _moe_kernel(elo_ref, ehi_ref, src_ref, srcn_ref, dst_ref, x_hbm, w13l_ref, w13h_ref, w2l_ref, w2h_ref,
                y_hbm, xbuf, ybuf, gsem, ssem, *, ts, d, n_tiles):
    del elo_ref, ehi_ref
    i = pl.program_id(0)
    slot = i % 2
    nslot = 1 - slot

    def start_gather(idx_ref, sl):
        for r in range(ts):
            pltpu.make_async_copy(x_hbm.at[pl.ds(idx_ref[0, r], 1)], xbuf.at[sl, pl.ds(r, 1)], gsem.at[sl]).start()

    def wait_gather(sl):
        pltpu.make_async_copy(x_hbm.at[pl.ds(0, ts)], xbuf.at[sl], gsem.at[sl]).wait()

    def wait_scatter(sl):
        pltpu.make_async_copy(ybuf.at[sl], y_hbm.at[pl.ds(0, ts)], ssem.at[sl]).wait()

    @pl.when(i == 0)
    def _():
        start_gather(src_ref, 0)

    wait_gather(slot)

    @pl.when(i >= 2)
    def _():
        wait_scatter(slot)

    start_gather(srcn_ref, nslot)

    xin = xbuf[slot]
    hv = xin[:, :d].astype(BF16)
    g_lower = xin[:, d + 1:d + 2]
    g_higher = xin[:, d + 2:d + 3]
    f = w2l_ref.shape[0]

    def hidden(w13_ref, gate):
        au = jnp.dot(hv, w13_ref[...], preferred_element_type=F32)
        return ((_silu(au[:, :f]) * au[:, f:]) * gate).astype(BF16)

    ybuf[slot] = (jnp.dot(hidden(w13l_ref, g_lower), w2l_ref[...], preferred_element_type=F32)
                  + jnp.dot(hidden(w13h_ref, g_higher), w2h_ref[...], preferred_element_type=F32))

    for r in range(ts):
        pltpu.make_async_copy(ybuf.at[slot, pl.ds(r, 1)], y_hbm.at[pl.ds(dst_ref[0, r], 1)], ssem.at[slot]).start()

    @pl.when(i == n_tiles - 1)
    def _():
        wait_scatter(slot)
        wait_scatter(nslot)
        wait_gather(nslot)


def _moe_plan(info, *, ts, n_pad_rows):
    t = info.shape[0]
    n_buckets = N_EXPERT_GROUPS * PAIRS_PER_GROUP
    bucket = info[:, 0].astype(jnp.int32)
    onehot = (bucket[:, None] == jnp.arange(n_buckets, dtype=jnp.int32)[None, :]).astype(jnp.int32)
    csum = jnp.cumsum(onehot, axis=0)
    rank = jnp.sum(csum * onehot, axis=1) - 1
    counts = csum[-1]
    padded = ((counts + ts - 1) // ts) * ts
    ends = jnp.cumsum(padded)
    starts = ends - padded
    pos = jnp.sum(starts[None, :] * onehot, axis=1) + rank
    tok = jnp.arange(t, dtype=jnp.int32)
    src = jnp.zeros((n_pad_rows,), jnp.int32).at[pos].set(tok)
    valid = jnp.zeros((n_pad_rows,), jnp.int32).at[pos].set(1)
    pad_rank = jnp.cumsum(1 - valid) - 1
    dst = jnp.where(valid == 1, src, t + pad_rank).astype(jnp.int32)
    n_tiles = n_pad_rows // ts
    tile_bucket = jnp.sum((jnp.arange(n_tiles, dtype=jnp.int32)[:, None] * ts >= ends[None, :]).astype(jnp.int32),
                          axis=1)
    tile_bucket = jnp.minimum(tile_bucket, n_buckets - 1)
    pair = tile_bucket % PAIRS_PER_GROUP
    pair_lo = jnp.array([0, 0, 0, 1, 1, 2], jnp.int32)[pair]
    pair_hi = jnp.array([1, 2, 3, 2, 3, 3], jnp.int32)[pair]
    base = (tile_bucket // PAIRS_PER_GROUP) * EXPERTS_PER_GROUP
    return (src.reshape(n_tiles, 1, ts), dst.reshape(n_tiles, 1, ts),
            (base + pair_lo).astype(jnp.int32), (base + pair_hi).astype(jnp.int32))


def _moe_call(hx, info, w13, w2):
    t, dx = hx.shape
    d = dx - LANES
    f = w2.shape[1]
    ts = MOE_TILE
    n_pad_rows = t + N_EXPERT_GROUPS * PAIRS_PER_GROUP * ts
    n_tiles = n_pad_rows // ts
    src, dst, e_lo, e_hi = _moe_plan(info, ts=ts, n_pad_rows=n_pad_rows)
    smem_tile = lambda fn: pl.BlockSpec((None, 1, ts), fn, memory_space=pltpu.SMEM)
    w_in = lambda pick: pl.BlockSpec((None, d, 2 * f), lambda i, lo, hi: (pick(lo, hi)[i], 0, 0))
    w_out = lambda pick: pl.BlockSpec((None, f, d), lambda i, lo, hi: (pick(lo, hi)[i], 0, 0))
    lower = lambda lo, hi: lo
    higher = lambda lo, hi: hi
    grid_spec = pltpu.PrefetchScalarGridSpec(
        num_scalar_prefetch=2,
        grid=(n_tiles,),
        in_specs=[
            smem_tile(lambda i, lo, hi: (i, 0, 0)),
            smem_tile(lambda i, lo, hi: (jnp.minimum(i + 1, n_tiles - 1), 0, 0)),
            smem_tile(lambda i, lo, hi: (i, 0, 0)),
            pl.BlockSpec(memory_space=pl.ANY),
            w_in(lower), w_in(higher), w_out(lower), w_out(higher),
        ],
        out_specs=pl.BlockSpec(memory_space=pl.ANY),
        scratch_shapes=[pltpu.VMEM((2, ts, dx), F32), pltpu.VMEM((2, ts, d), F32),
                        pltpu.SemaphoreType.DMA((2,)), pltpu.SemaphoreType.DMA((2,))],
    )
    return pl.pallas_call(
        functools.partial(_moe_kernel, ts=ts, d=d, n_tiles=n_tiles),
        grid_spec=grid_spec,
        out_shape=jax.ShapeDtypeStruct((n_pad_rows, d), F32),
        compiler_params=_params(("arbitrary",)),
        name="moe_routed",
    )(e_lo, e_hi, src, src, dst, hx, w13, w13, w2, w2)


def _final_kernel(x_ref, y_ref, gmod_ref, fw_ref, o_ref):
    xv = x_ref[...] + gmod_ref[5:6, :] * y_ref[...]
    r = lax.rsqrt(jnp.mean(xv * xv, axis=-1, keepdims=True) + RMS_EPS)
    o_ref[...] = (xv * r) * fw_ref[...]


def _final_call(x, y, gmod, fw, *, n_lat):
    _, d = x.shape
    tm = 256
    row = pl.BlockSpec((tm, d), lambda i: (i, 0))
    full = lambda a: pl.BlockSpec(a.shape, lambda i: (0,) * a.ndim)
    return pl.pallas_call(
        _final_kernel,
        grid=(n_lat // tm,),
        in_specs=[row, row, full(gmod), full(fw)],
        out_specs=row,
        out_shape=jax.ShapeDtypeStruct((n_lat, d), F32),
        compiler_params=_params(("parallel",)),
        name="final_norm",
    )(x, y, gmod, fw)


def _rope_tables(n_lat, n_ctx):
    axis_dim = DIFF_HEAD_DIM // 2
    quarter = axis_dim // 2
    rows = n_lat // GRID_W
    row = jnp.broadcast_to(jnp.arange(rows, dtype=F32)[:, None], (rows, GRID_W)).reshape(-1)
    col = jnp.broadcast_to(jnp.arange(GRID_W, dtype=F32)[None, :], (rows, GRID_W)).reshape(-1)
    inv_freq = ROPE_THETA ** (-jnp.arange(0, axis_dim, 2, dtype=F32) / axis_dim)
    ang_r = row[:, None] * inv_freq
    ang_c = col[:, None] * inv_freq
    zeros = jnp.zeros((n_lat, quarter), F32)
    c = jnp.concatenate([jnp.cos(ang_r), jnp.cos(ang_r), jnp.cos(ang_c), jnp.cos(ang_c)], axis=1)
    sa = jnp.concatenate([-jnp.sin(ang_r), zeros, -jnp.sin(ang_c), zeros], axis=1)
    sb = jnp.concatenate([zeros, jnp.sin(ang_r), zeros, jnp.sin(ang_c)], axis=1)
    pad = lambda a, v: jnp.concatenate([a, jnp.full((n_ctx, DIFF_HEAD_DIM), v, F32)], axis=0)
    return pad(c, 1.0), pad(sa, 0.0), pad(sb, 0.0)


def _pack_mod(m, d):
    z = jnp.zeros((MOD_CTX_ROW - N_MOD, d), F32)
    return jnp.concatenate([m[0].reshape(N_MOD, d), z, m[1].reshape(N_MOD, d), z], axis=0)


def kernel(x, c, ctx, c_ctx, ada_w, ada_b, norm_w, pool_w, pool_b, pool_scale, attn_w_qkv, attn_w_o,
           attn_lambda, attn_subln_w, router_group_w, router_expert_w, expert_w1, expert_w3, expert_w2,
           final_norm_w):
    _, n_lat, d = x.shape
    n_ctx = ctx.shape[1]
    depth = ada_w.shape[0]
    assert x.shape[0] == 1 and n_lat % n_ctx == 0 and n_ctx % 256 == 0 and d % (4 * LANES) == 0

    xs = jnp.concatenate([x[0], ctx[0]], axis=0)
    mods = _adaln(c, c_ctx, ada_w, ada_b)
    mod = [_pack_mod(mods[i], d) for i in range(depth)]

    wr = jnp.concatenate([router_group_w, router_expert_w], axis=-1)
    wr = jnp.pad(wr, ((0, 0), (0, 0), (0, LANES - wr.shape[-1])))
    wr_hi = wr.astype(BF16)
    wr_lo = (wr - wr_hi.astype(F32)).astype(BF16)
    wr2 = jnp.concatenate([wr_hi, wr_lo], axis=-1)

    pool_w16 = pool_w.astype(BF16)
    wqkv16 = attn_w_qkv.astype(BF16)
    wo16 = attn_w_o.astype(BF16)
    w13_16 = jnp.concatenate([expert_w1.astype(BF16), expert_w3.astype(BF16)], axis=-1)
    w2_16 = expert_w2.astype(BF16)
    rope_c, rope_sa, rope_sb = _rope_tables(n_lat, n_ctx)

    y = None
    for i in range(depth):
        jm = i // 2
        gprev = mod[i - 1] if i > 0 else None
        if i % 2 == 0:
            xs, h2, gates = _pool_call(xs, y, gprev, mod[i], norm_w[i], pool_w16[jm], pool_b[jm],
                                       pool_scale[jm].reshape(1, d), wr2[i], n_lat=n_lat)
        else:
            lambda_init = 0.8 - 0.6 * math.exp(-0.3 * i)
            xs, h, _ = _norm_call(xs, y, gprev, mod[i], norm_w[i, 0:1], None, k_shift=0, n_lat=n_lat)
            qk, vt = _qkv_call(h, wqkv16[jm], rope_c, rope_sa, rope_sb)
            o = _attn_call(qk, vt, attn_lambda[jm], attn_subln_w[jm], lambda_init=lambda_init,
                           n_lat=n_lat, n_ctx=n_ctx)
            xs = _oproj_call(o, wo16[jm], xs, mod[i], n_lat=n_lat)
            _, h2, gates = _norm_call(xs, None, None, mod[i], norm_w[i, 1:2], wr2[i], k_shift=3, n_lat=n_lat)
        y = _moe_call(h2, gates, w13_16[i], w2_16[i])
    out = _final_call(xs, y, mod[depth - 1], final_norm_w.reshape(1, d), n_lat=n_lat)
    return out[None]
```

```python
import functools
import math

import jax
import jax.numpy as jnp
from jax import lax
from jax.experimental import pallas as pl
from jax.experimental.pallas import tpu as pltpu

F32 = jnp.float32
BF16 = jnp.bfloat16

GRID_W = 64
RMS_EPS = 1e-6
ROPE_THETA = 10000.0
POOL_WINDOWS = (2, 4, 8, 16)
N_MOD = 6
N_EXPERT_GROUPS = 4
EXPERTS_PER_GROUP = 4
N_EXPERTS = N_EXPERT_GROUPS * EXPERTS_PER_GROUP
DIFF_HEAD_DIM = 128
HEAD_W = 2 * DIFF_HEAD_DIM
LANES = 128
SUBLANES = 8
MOD_CTX_ROW = 8
ROUTER_COL0 = N_EXPERT_GROUPS
PAIRS_PER_GROUP = 6
MOE_TILE = 128
ADALN_COLS = 2048
VMEM_LIMIT = 56 * 1024 * 1024
NEG_BIG = -1e30
LOG2E = 1.4426950408889634


def _pick(n, candidates):
    for c in candidates:
        if c <= n and n % c == 0:
            return c
    return n


def _params(semantics):
    return pltpu.CompilerParams(dimension_semantics=semantics, vmem_limit_bytes=VMEM_LIMIT)


def _layer_spec(stacked, layer):
    return pl.BlockSpec((None,) + stacked.shape[1:], lambda *_: (layer,) + (0,) * (stacked.ndim - 1))


def _silu(v):
    return v * jax.nn.sigmoid(v)


def _adaln_kernel(c_ref, cc_ref, w_ref, b_ref, o_ref, acc_ref, s_ref, *, tk, tn, nk):
    k = pl.program_id(2)

    @pl.when(k == 0)
    def _():
        acc_ref[...] = jnp.zeros_like(acc_ref)

    s_ref[0] = _silu(c_ref[...])
    s_ref[1] = _silu(cc_ref[...])

    cw = min(tn, ADALN_COLS)
    reps = cw // LANES
    for c0 in range(0, tn, cw):
        def body(j, carry, c0=c0):
            a0, a1 = carry
            r = pl.multiple_of(j * SUBLANES, SUBLANES)
            w = w_ref[pl.ds(r, SUBLANES), c0:c0 + cw]
            a0 = a0 + w * jnp.concatenate([s_ref[0, pl.ds(r, SUBLANES), :]] * reps, axis=1)
            a1 = a1 + w * jnp.concatenate([s_ref[1, pl.ds(r, SUBLANES), :]] * reps, axis=1)
            return a0, a1

        z = jnp.zeros((SUBLANES, cw), F32)
        a0, a1 = lax.fori_loop(0, tk // SUBLANES, body, (z, z), unroll=4)
        acc_ref[0, :, c0:c0 + cw] += a0
        acc_ref[1, :, c0:c0 + cw] += a1

    @pl.when(k == nk - 1)
    def _():
        o_ref[0:1, :] = jnp.sum(acc_ref[0], axis=0, keepdims=True) + b_ref[...]
        o_ref[1:2, :] = jnp.sum(acc_ref[1], axis=0, keepdims=True) + b_ref[...]


def _adaln(c, c_ctx, ada_w, ada_b):
    depth, d, n = ada_w.shape
    tk = _pick(d, (512, 256, 128))
    tn = _pick(n, (4096, 2048, 1024, 512, 256, 128))
    nk = d // tk
    c_b = jnp.broadcast_to(c.reshape(d, 1), (d, LANES))
    cc_b = jnp.broadcast_to(c_ctx.reshape(d, 1), (d, LANES))
    return pl.pallas_call(
        functools.partial(_adaln_kernel, tk=tk, tn=tn, nk=nk),
        grid=(depth, n // tn, nk),
        in_specs=[
            pl.BlockSpec((tk, LANES), lambda l, j, k: (k, 0)),
            pl.BlockSpec((tk, LANES), lambda l, j, k: (k, 0)),
            pl.BlockSpec((None, tk, tn), lambda l, j, k: (l, k, j)),
            pl.BlockSpec((None, 1, tn), lambda l, j, k: (l, 0, j)),
        ],
        out_specs=pl.BlockSpec((None, 2, tn), lambda l, j, k: (l, 0, j)),
        out_shape=jax.ShapeDtypeStruct((depth, 2, n), F32),
        scratch_shapes=[pltpu.VMEM((2, SUBLANES, tn), F32), pltpu.VMEM((2, tk, LANES), F32)],
        compiler_params=_params(("parallel", "parallel", "arbitrary")),
        name="adaln",
    )(c_b, cc_b, ada_w, ada_b.reshape(depth, 1, n))


def _mod_row(mod_ref, k, is_ctx):
    return jnp.where(is_ctx, mod_ref[MOD_CTX_ROW + k:MOD_CTX_ROW + k + 1, :], mod_ref[k:k + 1, :])


def _norm_mod(xv, w_row, shift, scale):
    r = lax.rsqrt(jnp.mean(xv * xv, axis=-1, keepdims=True) + RMS_EPS)
    return (xv * r) * (w_row * (1.0 + scale)) + shift


def _router_info(h, h_hi, wr_ref):
    h_lo = (h - h_hi.astype(F32)).astype(BF16)
    r1 = jnp.dot(h_hi, wr_ref[...], preferred_element_type=F32)
    r2 = jnp.dot(h_lo, wr_ref[:, :LANES], preferred_element_type=F32)
    logits = r1[:, :LANES] + r1[:, LANES:] + r2
    col = lax.broadcasted_iota(jnp.int32, logits.shape, 1).astype(F32)
    far = float(4 * LANES)
    is_g = col < float(N_EXPERT_GROUPS)
    mg = jnp.max(jnp.where(is_g, logits, NEG_BIG), axis=1, keepdims=True)
    denom = jnp.sum(jnp.where(is_g, jnp.exp(logits - mg), 0.0), axis=1, keepdims=True)
    pg_top = 1.0 / denom
    g_sel = jnp.min(jnp.where(is_g & (logits == mg), col, far), axis=1, keepdims=True)
    e_lo = float(ROUTER_COL0) + float(EXPERTS_PER_GROUP) * g_sel
    in_grp = (col >= e_lo) & (col < e_lo + float(EXPERTS_PER_GROUP))
    v1 = jnp.max(jnp.where(in_grp, logits, NEG_BIG), axis=1, keepdims=True)
    i1 = jnp.min(jnp.where(in_grp & (logits == v1), col, far), axis=1, keepdims=True)
    rest = in_grp & (col != i1)
    v2 = jnp.max(jnp.where(rest, logits, NEG_BIG), axis=1, keepdims=True)
    i2 = jnp.min(jnp.where(rest & (logits == v2), col, far), axis=1, keepdims=True)
    e2 = jnp.exp(v2 - v1)
    w1 = pg_top / (1.0 + e2)
    w2 = pg_top * (e2 / (1.0 + e2))
    first_lower = i1 < i2
    ja = jnp.minimum(i1, i2) - e_lo
    jb = jnp.maximum(i1, i2) - e_lo
    bucket = g_sel * float(PAIRS_PER_GROUP) + ja * (7.0 - ja) * 0.5 + (jb - ja - 1.0)
    g_lower = jnp.where(first_lower, w1, w2)
    g_higher = jnp.where(first_lower, w2, w1)
    return jnp.where(col == 0.0, bucket, jnp.where(col == 1.0, g_lower, jnp.where(col == 2.0, g_higher, 0.0)))


def _emit_routed_rows(h, wr_ref, hx_ref, info_ref):
    d = h.shape[1]
    info = _router_info(h, h.astype(BF16), wr_ref)
    hx_ref[:, :d] = h
    hx_ref[:, d:] = info
    info_ref[...] = info


def _norm_kernel(*refs, has_y, router, k_shift, lat_tiles):
    it = iter(refs)
    x_ref = next(it)
    y_ref = next(it) if has_y else None
    gmod_ref = next(it) if has_y else None
    mod_ref = next(it)
    nw_ref = next(it)
    wr_ref = next(it) if router else None
    xo_ref = next(it) if has_y else None
    h_ref = next(it)
    g_ref = next(it) if router else None

    is_ctx = pl.program_id(0) >= lat_tiles
    xv = x_ref[...]
    if has_y:
        xv = xv + _mod_row(gmod_ref, 5, is_ctx) * y_ref[...]
        xo_ref[...] = xv
    h = _norm_mod(xv, nw_ref[...], _mod_row(mod_ref, k_shift, is_ctx), _mod_row(mod_ref, k_shift + 1, is_ctx))
    if router:
        _emit_routed_rows(h, wr_ref, h_ref, g_ref)
    else:
        h_ref[...] = h.astype(BF16)


def _norm_call(x, y, gmod, mod, nw, wr, *, k_shift, n_lat, layer=0):
    t, d = x.shape
    tm = 256
    has_y = y is not None
    router = wr is not None
    row = pl.BlockSpec((tm, d), lambda i: (i, 0))
    full = lambda a: pl.BlockSpec(a.shape, lambda i: (0,) * a.ndim)
    ins, specs = [x], [row]
    if has_y:
        ins += [y, gmod]
        specs += [row, full(gmod)]
    ins += [mod, nw]
    specs += [full(mod), full(nw)]
    if router:
        ins.append(wr)
        specs.append(_layer_spec(wr, layer))
    outs, ospecs = [], []
    if has_y:
        outs.append(jax.ShapeDtypeStruct((t, d), F32))
        ospecs.append(row)
    if router:
        outs += [jax.ShapeDtypeStruct((t, d + LANES), F32), jax.ShapeDtypeStruct((t, LANES), F32)]
        ospecs += [pl.BlockSpec((tm, d + LANES), lambda i: (i, 0)), pl.BlockSpec((tm, LANES), lambda i: (i, 0))]
    else:
        outs.append(jax.ShapeDtypeStruct((t, d), BF16))
        ospecs.append(row)
    res = pl.pallas_call(
        functools.partial(_norm_kernel, has_y=has_y, router=router, k_shift=k_shift, lat_tiles=n_lat // tm),
        grid=(t // tm,),
        in_specs=specs,
        out_specs=ospecs,
        out_shape=outs,
        compiler_params=_params(("parallel",)),
        name="norm_mod",
    )(*ins)
    res = list(res)
    xo = res.pop(0) if has_y else x
    h = res.pop(0)
    g = res.pop(0) if router else None
    return xo, h, g


HALO = 8


def _pool_kernel(*refs, has_y, split_in, tp, cg, lat_tiles, n_tiles):
    it = iter(refs)
    x_ref, xp_ref, xn_ref = next(it), next(it), next(it)
    if split_in:
        c_ref, cp_ref, cn_ref = next(it), next(it), next(it)
    if has_y:
        y_ref, yp_ref, yn_ref, gmod_ref = next(it), next(it), next(it), next(it)
    mod_ref, nw_ref, pw_ref, pb_ref, ps_ref, wr_ref = (next(it) for _ in range(6))
    x1_ref, h2_ref, g_ref = next(it), next(it), next(it)
    hext_ref, xin_ref = next(it), next(it)

    j = pl.program_id(0)
    is_ctx = j >= lat_tiles
    first = (j == 0) | (j == lat_tiles)
    last = (j == lat_tiles - 1) | (j == n_tiles - 1)

    nw = nw_ref[0:1, :]
    ws = nw * (1.0 + _mod_row(mod_ref, 1, is_ctx))
    shift = _mod_row(mod_ref, 0, is_ctx)

    def x_in(xr, yr, cr):
        xv = xr[...]
        if split_in:
            xv = jnp.where(is_ctx, cr[...], xv)
        if has_y:
            xv = xv + _mod_row(gmod_ref, 5, is_ctx) * yr[...]
        return xv

    def h_of(xv):
        r = lax.rsqrt(jnp.mean(xv * xv, axis=-1, keepdims=True) + RMS_EPS)
        return (xv * r) * ws + shift

    xm = x_in(x_ref, y_ref if has_y else None, c_ref if split_in else None)
    xin_ref[...] = xm
    hext_ref[HALO:HALO + tp, :] = h_of(xm)
    hp = h_of(x_in(xp_ref, yp_ref if has_y else None, cp_ref if split_in else None))
    hext_ref[0:HALO, :] = jnp.where(first, 0.0, hp)
    hn = h_of(x_in(xn_ref, yn_ref if has_y else None, cn_ref if split_in else None))
    hext_ref[HALO + tp:HALO + tp + HALO, :] = jnp.where(last, 0.0, hn)

    tpos = lax.broadcasted_iota(jnp.int32, (tp, 1), 0).astype(F32)
    lo_lim = jnp.where(first, 0.0, -float(2 * HALO))
    hi_lim = jnp.where(last, float(tp), float(tp + 2 * HALO))
    gate = _mod_row(mod_ref, 2, is_ctx)
    for g, w in enumerate(POOL_WINDOWS):
        c0 = g * cg
        half = w // 2
        wsum = hext_ref[pl.ds(HALO - half, tp), c0:c0 + cg]
        for dd in range(-half + 1, half):
            wsum = wsum + hext_ref[pl.ds(HALO + dd, tp), c0:c0 + cg]
        cnt = jnp.minimum(tpos + float(half), hi_lim) - jnp.maximum(tpos - float(half), lo_lim)
        pooled = wsum / cnt - hext_ref[HALO:HALO + tp, c0:c0 + cg]
        yg = jnp.dot(pooled.astype(BF16), pw_ref[g], preferred_element_type=F32) + pb_ref[g:g + 1, :]
        yg = yg * ps_ref[:, c0:c0 + cg]
        x1_ref[:, c0:c0 + cg] = xin_ref[:, c0:c0 + cg] + gate[:, c0:c0 + cg] * yg

    x1 = x1_ref[...]
    h2 = _norm_mod(x1, nw_ref[1:2, :], _mod_row(mod_ref, 3, is_ctx), _mod_row(mod_ref, 4, is_ctx))
    _emit_routed_rows(h2, wr_ref, h2_ref, g_ref)


def _pool_call(x, y, gmod, mod, nw2, pw, pb, ps, wr, *, n_lat, layer, mixer, ctx=None):
    d = x.shape[1]
    split_in = ctx is not None
    t = x.shape[0] + (ctx.shape[0] if split_in else 0)
    tp = 128
    cg = d // len(POOL_WINDOWS)
    has_y = y is not None
    n_tiles = t // tp
    lat_tiles = n_lat // tp
    bpt = tp // HALO

    def halo_specs(n_rows, tile0):
        n_t, last_blk = n_rows // tp, n_rows // HALO - 1
        clip = lambda v, hi: jnp.minimum(jnp.maximum(v, 0), hi)
        return [pl.BlockSpec((tp, d), lambda i: (clip(i - tile0, n_t - 1), 0)),
                pl.BlockSpec((HALO, d), lambda i: (clip((i - tile0) * bpt - 1, last_blk), 0)),
                pl.BlockSpec((HALO, d), lambda i: (clip((i - tile0 + 1) * bpt, last_blk), 0))]

    row = pl.BlockSpec((tp, d), lambda i: (i, 0))
    full = lambda a: pl.BlockSpec(a.shape, lambda i: (0,) * a.ndim)
    ins, specs = [x, x, x], halo_specs(x.shape[0], 0)
    if split_in:
        ins += [ctx, ctx, ctx]
        specs += halo_specs(ctx.shape[0], lat_tiles)
    _, prev, nxt = halo_specs(t, 0)
    if has_y:
        ins += [y, y, y, gmod]
        specs += [row, prev, nxt, full(gmod)]
    ins += [mod, nw2, pw, pb, ps, wr]
    specs += [full(mod), full(nw2), _layer_spec(pw, mixer), full(pb), full(ps), _layer_spec(wr, layer)]
    return pl.pallas_call(
        functools.partial(_pool_kernel, has_y=has_y, split_in=split_in, tp=tp, cg=cg, lat_tiles=lat_tiles,
                          n_tiles=n_tiles),
        grid=(n_tiles,),
        in_specs=specs,
        out_specs=[row, pl.BlockSpec((tp, d + LANES), lambda i: (i, 0)), pl.BlockSpec((tp, LANES), lambda i: (i, 0))],
        out_shape=[jax.ShapeDtypeStruct((t, d), F32), jax.ShapeDtypeStruct((t, d + LANES), F32),
                   jax.ShapeDtypeStruct((t, LANES), F32)],
        scratch_shapes=[pltpu.VMEM((tp + 2 * HALO, d), F32), pltpu.VMEM((tp, d), F32)],
        compiler_params=_params(("parallel",)),
        name="pool_layer",
    )(*ins)


def _qk_kernel(a_ref, w_ref, c_ref, sa_ref, sb_ref, o_ref, *, d, tn, qscale):
    acc = jnp.dot(a_ref[...], w_ref[...], preferred_element_type=F32)
    f = jnp.where(pl.program_id(1) < d // tn, qscale, 1.0)
    cc = c_ref[...] * f
    sa = sa_ref[...] * f
    sb = sb_ref[...] * f
    for b in range(tn // LANES):
        blk = acc[:, b * LANES:(b + 1) * LANES]
        rot = blk * cc + pltpu.roll(blk, 96, 1) * sa + pltpu.roll(blk, 32, 1) * sb
        o_ref[:, b * LANES:(b + 1) * LANES] = rot.astype(BF16)


def _vt_kernel(a_ref, w_ref, o_ref, acc_ref):
    acc_ref[...] = jnp.dot(a_ref[...], w_ref[...], preferred_element_type=F32)
    o_ref[...] = acc_ref[...].T.astype(BF16)


def _qkv_call(h, w, mixer, rope_c, rope_sa, rope_sb):
    t, d = h.shape
    tm = _pick(t, (640, 512, 256, 128))
    tn = _pick(d, (1024, 512, 256))
    qscale = (DIFF_HEAD_DIM ** -0.5) * LOG2E
    tab = pl.BlockSpec((tm, LANES), lambda i, j: (i, 0))
    a_spec = pl.BlockSpec((tm, d), lambda i, j: (i, 0))
    qk = pl.pallas_call(
        functools.partial(_qk_kernel, d=d, tn=tn, qscale=qscale),
        grid=(t // tm, (2 * d) // tn),
        in_specs=[a_spec, pl.BlockSpec((None, d, tn), lambda i, j: (mixer, 0, j)), tab, tab, tab],
        out_specs=pl.BlockSpec((tm, tn), lambda i, j: (i, j)),
        out_shape=jax.ShapeDtypeStruct((t, 2 * d), BF16),
        compiler_params=_params(("parallel", "arbitrary")),
        name="qk_proj",
    )(h, w, rope_c, rope_sa, rope_sb)
    v_col0 = (2 * d) // tn
    vt = pl.pallas_call(
        _vt_kernel,
        grid=(t // tm, d // tn),
        in_specs=[a_spec, pl.BlockSpec((None, d, tn), lambda i, j: (mixer, 0, v_col0 + j))],
        out_specs=pl.BlockSpec((tn, tm), lambda i, j: (j, i)),
        out_shape=jax.ShapeDtypeStruct((d, t), BF16),
        scratch_shapes=[pltpu.VMEM((tm, tn), F32)],
        compiler_params=_params(("parallel", "arbitrary")),
        name="v_proj_t",
    )(h, w)
    return qk, vt


def _oproj_kernel(a_ref, w_ref, x_ref, mod_ref, o_ref, *, tm, n_lat):
    acc = jnp.dot(a_ref[...], w_ref[...], preferred_element_type=F32)
    rows = pl.program_id(0) * tm + lax.broadcasted_iota(jnp.int32, (tm, 1), 0)
    gate = jnp.where(rows >= n_lat, mod_ref[MOD_CTX_ROW + 2:MOD_CTX_ROW + 3, :], mod_ref[2:3, :])
    o_ref[...] = x_ref[...] + gate * acc


def _oproj_call(o, w, mixer, x, mod, *, n_lat):
    t, d = x.shape
    tm = _pick(t, (640, 512, 256, 128))
    tn = _pick(d, (1024, 512, 256))
    return pl.pallas_call(
        functools.partial(_oproj_kernel, tm=tm, n_lat=n_lat),
        grid=(t // tm, d // tn),
        in_specs=[pl.BlockSpec((tm, d), lambda i, j: (i, 0)),
                  pl.BlockSpec((None, d, tn), lambda i, j: (mixer, 0, j)),
                  pl.BlockSpec((tm, tn), lambda i, j: (i, j)), pl.BlockSpec((16, tn), lambda i, j: (0, j))],
        out_specs=pl.BlockSpec((tm, tn), lambda i, j: (i, j)),
        out_shape=jax.ShapeDtypeStruct((t, d), F32),
        compiler_params=_params(("parallel", "arbitrary")),
        name="attn_out_proj",
    )(o, w, x, mod)


def _diff_lambda(lam_ref, lambda_init):
    lv = lam_ref[...]
    a = jnp.sum(lv[0:1, :] * lv[1:2, :], axis=1, keepdims=True)
    b = jnp.sum(lv[2:3, :] * lv[3:4, :], axis=1, keepdims=True)
    return jnp.exp(a) - jnp.exp(b) + lambda_init


def _attn_scores_t(q_ref, k_ref, c):
    qc = q_ref[:, c * DIFF_HEAD_DIM:(c + 1) * DIFF_HEAD_DIM]
    kc = k_ref[:, c * DIFF_HEAD_DIM:(c + 1) * DIFF_HEAD_DIM]
    return lax.dot_general(kc, qc, (((1,), (1,)), ((), ())), preferred_element_type=F32)


def _attn_finish(ot, sw_ref, lambda_init):
    r = lax.rsqrt(jnp.mean(ot * ot, axis=0, keepdims=True) + RMS_EPS)
    return (((ot * r).T * sw_ref[...]) * (1.0 - lambda_init)).astype(BF16)


def _attn_kernel(q_ref, k_ref, vt_ref, octx_ref, lam_ref, sw_ref, o_ref, m_ref, l_ref, acc_ref,
                 *, lambda_init, nq, nk, qp):
    qi = pl.program_id(1)
    ki = pl.program_id(2)

    @pl.when(qi < nq)
    def _():
        _attn_step(q_ref, k_ref, vt_ref, lam_ref, sw_ref, o_ref, m_ref, l_ref, acc_ref, ki,
                   lambda_init=lambda_init, nk=nk, qp=qp)

    @pl.when((qi == nq) & (ki == nk - 1))
    def _():
        n_ctx = octx_ref.shape[0]
        o_ref[0:n_ctx, :] = octx_ref[...]
        o_ref[n_ctx:, :] = jnp.zeros((o_ref.shape[0] - n_ctx, o_ref.shape[1]), o_ref.dtype)


def _attn_step(q_ref, k_ref, vt_ref, lam_ref, sw_ref, o_ref, m_ref, l_ref, acc_ref, ki, *, lambda_init, nk, qp):
    @pl.when(ki == 0)
    def _():
        m_ref[...] = jnp.full(m_ref.shape, NEG_BIG, F32)
        l_ref[...] = jnp.zeros_like(l_ref)
        acc_ref[...] = jnp.zeros_like(acc_ref)

    tq = q_ref.shape[0]
    chains = [(c, j) for j in range(tq // qp) for c in range(2)]

    def scores(c, j):
        qc = q_ref[j * qp:(j + 1) * qp, c * DIFF_HEAD_DIM:(c + 1) * DIFF_HEAD_DIM]
        kc = k_ref[:, c * DIFF_HEAD_DIM:(c + 1) * DIFF_HEAD_DIM]
        return lax.dot_general(kc, qc, (((1,), (1,)), ((), ())), preferred_element_type=F32)

    s_next = scores(*chains[0])
    for n, (c, j) in enumerate(chains):
        s = s_next
        if n + 1 < len(chains):
            s_next = scores(*chains[n + 1])
        cols = slice(j * qp, (j + 1) * qp)
        m_prev = m_ref[c, :, cols]
        m_new = jnp.maximum(m_prev, jnp.max(s, axis=0, keepdims=True))
        alpha = jnp.exp2(m_prev - m_new)
        p = jnp.exp2(s - m_new)
        l_ref[c, :, cols] = alpha * l_ref[c, :, cols] + jnp.sum(p, axis=0, keepdims=True)
        acc_ref[c, :, cols] = alpha * acc_ref[c, :, cols] + jnp.dot(
            vt_ref[...], p.astype(BF16), preferred_element_type=F32)
        m_ref[c, :, cols] = m_new

    @pl.when(ki == nk - 1)
    def _():
        lam = _diff_lambda(lam_ref, lambda_init)
        ot = acc_ref[0] / l_ref[0] - lam * (acc_ref[1] / l_ref[1])
        o_ref[...] = _attn_finish(ot, sw_ref, lambda_init)


def _attn_ctx_kernel(q_ref, k_ref, vt_ref, lam_ref, sw_ref, o_ref, *, lambda_init):
    lam = _diff_lambda(lam_ref, lambda_init)
    outs = []
    for c in range(2):
        s = _attn_scores_t(q_ref, k_ref, c)
        p = jnp.exp2(s - jnp.max(s, axis=0, keepdims=True))
        l = jnp.sum(p, axis=0, keepdims=True)
        outs.append(jnp.dot(vt_ref[...], p.astype(BF16), preferred_element_type=F32) / l)
    o_ref[...] = _attn_finish(outs[0] - lam * outs[1], sw_ref, lambda_init)


def _attn_call(qk, vt, lam_vecs, subln_w, *, lambda_init, n_lat, n_ctx):
    d, t = vt.shape
    heads = d // HEAD_W
    tq = _pick(n_lat, (2048, 1024, 512, 256, 128))
    tk = _pick(t, (3328, 1280, 640, 256, 128))
    nq = n_lat // tq
    nk = t // tk
    cblk = n_lat // n_ctx
    assert n_ctx <= tq
    sw = subln_w.reshape(1, HEAD_W)
    small = lambda a: pl.BlockSpec(a.shape, lambda *_: (0,) * a.ndim)
    o_ctx = pl.pallas_call(
        functools.partial(_attn_ctx_kernel, lambda_init=lambda_init),
        grid=(heads,),
        in_specs=[
            pl.BlockSpec((n_ctx, HEAD_W), lambda h: (cblk, h)),
            pl.BlockSpec((n_ctx, HEAD_W), lambda h: (cblk, heads + h)),
            pl.BlockSpec((HEAD_W, n_ctx), lambda h: (h, cblk)),
            small(lam_vecs), small(sw),
        ],
        out_specs=pl.BlockSpec((n_ctx, HEAD_W), lambda h: (0, h)),
        out_shape=jax.ShapeDtypeStruct((n_ctx, d), BF16),
        compiler_params=_params(("parallel",)),
        name="diff_attn_ctx",
    )(qk, qk, vt, lam_vecs, sw)
    last_q = nq - 1
    kv_blk = lambda i, k: jnp.where(i < nq, k, nk - 1)
    return pl.pallas_call(
        functools.partial(_attn_kernel, lambda_init=lambda_init, nq=nq, nk=nk, qp=min(tq, 256)),
        grid=(heads, nq + 1, nk),
        in_specs=[
            pl.BlockSpec((tq, HEAD_W), lambda h, i, k: (jnp.minimum(i, last_q), h)),
            pl.BlockSpec((tk, HEAD_W), lambda h, i, k: (kv_blk(i, k), heads + h)),
            pl.BlockSpec((HEAD_W, tk), lambda h, i, k: (h, kv_blk(i, k))),
            pl.BlockSpec((n_ctx, HEAD_W), lambda h, i, k: (0, h)),
            small(lam_vecs), small(sw),
        ],
        out_specs=pl.BlockSpec((tq, HEAD_W), lambda h, i, k: (i, h)),
        out_shape=jax.ShapeDtypeStruct(((nq + 1) * tq, d), BF16),
        scratch_shapes=[pltpu.VMEM((2, 1, tq), F32), pltpu.VMEM((2, 1, tq), F32),
                        pltpu.VMEM((2, HEAD_W, tq), F32)],
        compiler_params=_params(("parallel", "parallel", "arbitrary")),
        name="diff_attn",
    )(qk, qk, vt, o_ctx, lam_vecs, sw)


def _moe_kernel(elo_ref, ehi_ref, src_ref, srcn_ref, dst_ref, x_hbm, w13l_ref, w13h_ref, w2l_ref, w2h_ref,
                y_hbm, xbuf, ybuf, gsem, ssem, *, ts, d, n_tiles):
    del elo_ref, ehi_ref
    i = pl.program_id(0)
    slot = i % 2
    nslot = 1 - slot

    def start_gather(idx_ref, sl):
        for r in range(ts):
            pltpu.make_async_copy(x_hbm.at[pl.ds(idx_ref[0, r], 1)], xbuf.at[sl, pl.ds(r, 1)], gsem.at[sl]).start()

    def wait_gather(sl):
        pltpu.make_async_copy(x_hbm.at[pl.ds(0, ts)], xbuf.at[sl], gsem.at[sl]).wait()

    def wait_scatter(sl):
        pltpu.make_async_copy(ybuf.at[sl], y_hbm.at[pl.ds(0, ts)], ssem.at[sl]).wait()

    @pl.when(i == 0)
    def _():
        start_gather(src_ref, 0)

    wait_gather(slot)

    @pl.when(i >= 2)
    def _():
        wait_scatter(slot)

    start_gather(srcn_ref, nslot)

    xin = xbuf[slot]
    hv = xin[:, :d].astype(BF16)
    g_lower = xin[:, d + 1:d + 2]
    g_higher = xin[:, d + 2:d + 3]
    f = w2l_ref.shape[0]

    def hidden(w13_ref, gate):
        au = jnp.dot(hv, w13_ref[...], preferred_element_type=F32)
        return ((_silu(au[:, :f]) * au[:, f:]) * gate).astype(BF16)

    ybuf[slot] = (jnp.dot(hidden(w13l_ref, g_lower), w2l_ref[...], preferred_element_type=F32)
                  + jnp.dot(hidden(w13h_ref, g_higher), w2h_ref[...], preferred_element_type=F32))

    for r in range(ts):
        pltpu.make_async_copy(ybuf.at[slot, pl.ds(r, 1)], y_hbm.at[pl.ds(dst_ref[0, r], 1)], ssem.at[slot]).start()

    @pl.when(i == n_tiles - 1)
    def _():
        wait_scatter(slot)
        wait_scatter(nslot)
        wait_gather(nslot)


def _moe_plan(info, *, ts, n_pad_rows):
    t = info.shape[0]
    n_buckets = N_EXPERT_GROUPS * PAIRS_PER_GROUP
    bucket = info[:, 0].astype(jnp.int32)
    onehot = (bucket[:, None] == jnp.arange(n_buckets, dtype=jnp.int32)[None, :]).astype(jnp.int32)
    csum = jnp.cumsum(onehot, axis=0)
    rank = jnp.sum(csum * onehot, axis=1) - 1
    counts = csum[-1]
    padded = ((counts + ts - 1) // ts) * ts
    ends = jnp.cumsum(padded)
    starts = ends - padded
    pos = jnp.sum(starts[None, :] * onehot, axis=1) + rank
    tok = jnp.arange(t, dtype=jnp.int32)
    slot_tok = jnp.full((n_pad_rows,), -1, jnp.int32).at[pos].set(tok)
    valid = slot_tok >= 0
    src = jnp.maximum(slot_tok, 0)
    pad_rank = jnp.cumsum(1 - valid.astype(jnp.int32)) - 1
    dst = jnp.where(valid, slot_tok, t + pad_rank).astype(jnp.int32)
    n_tiles = n_pad_rows // ts
    tile_bucket = jnp.sum((jnp.arange(n_tiles, dtype=jnp.int32)[:, None] * ts >= ends[None, :]).astype(jnp.int32),
                          axis=1)
    tile_bucket = jnp.minimum(tile_bucket, n_buckets - 1)
    pair = tile_bucket % PAIRS_PER_GROUP
    pair_lo = jnp.array([0, 0, 0, 1, 1, 2], jnp.int32)[pair]
    pair_hi = jnp.array([1, 2, 3, 2, 3, 3], jnp.int32)[pair]
    base = (tile_bucket // PAIRS_PER_GROUP) * EXPERTS_PER_GROUP
    return (src.reshape(n_tiles, 1, ts), dst.reshape(n_tiles, 1, ts),
            (base + pair_lo).astype(jnp.int32), (base + pair_hi).astype(jnp.int32))


def _moe_call(hx, info, w13, w2, layer):
    t, dx = hx.shape
    d = dx - LANES
    f = w2.shape[2]
    ts = MOE_TILE
    n_pad_rows = t + N_EXPERT_GROUPS * PAIRS_PER_GROUP * ts
    n_tiles = n_pad_rows // ts
    src, dst, e_lo, e_hi = _moe_plan(info, ts=ts, n_pad_rows=n_pad_rows)
    smem_tile = lambda fn: pl.BlockSpec((None, 1, ts), fn, memory_space=pltpu.SMEM)
    w_in = lambda pick: pl.BlockSpec((None, None, d, 2 * f), lambda i, lo, hi: (layer, pick(lo, hi)[i], 0, 0))
    w_out = lambda pick: pl.BlockSpec((None, None, f, d), lambda i, lo, hi: (layer, pick(lo, hi)[i], 0, 0))
    lower = lambda lo, hi: lo
    higher = lambda lo, hi: hi
    grid_spec = pltpu.PrefetchScalarGridSpec(
        num_scalar_prefetch=2,
        grid=(n_tiles,),
        in_specs=[
            smem_tile(lambda i, lo, hi: (i, 0, 0)),
            smem_tile(lambda i, lo, hi: (jnp.minimum(i + 1, n_tiles - 1), 0, 0)),
            smem_tile(lambda i, lo, hi: (i, 0, 0)),
            pl.BlockSpec(memory_space=pl.ANY),
            w_in(lower), w_in(higher), w_out(lower), w_out(higher),
        ],
        out_specs=pl.BlockSpec(memory_space=pl.ANY),
        scratch_shapes=[pltpu.VMEM((2, ts, dx), F32), pltpu.VMEM((2, ts, d), F32),
                        pltpu.SemaphoreType.DMA((2,)), pltpu.SemaphoreType.DMA((2,))],
    )
    return pl.pallas_call(
        functools.partial(_moe_kernel, ts=ts, d=d, n_tiles=n_tiles),
        grid_spec=grid_spec,
        out_shape=jax.ShapeDtypeStruct((n_pad_rows, d), F32),
        compiler_params=_params(("arbitrary",)),
        name="moe_routed",
    )(e_lo, e_hi, src, src, dst, hx, w13, w13, w2, w2)


def _final_kernel(x_ref, y_ref, gmod_ref, fw_ref, o_ref):
    xv = x_ref[...] + gmod_ref[5:6, :] * y_ref[...]
    r = lax.rsqrt(jnp.mean(xv * xv, axis=-1, keepdims=True) + RMS_EPS)
    o_ref[...] = (xv * r) * fw_ref[...]


def _final_call(x, y, gmod, fw, *, n_lat):
    _, d = x.shape
    tm = 256
    row = pl.BlockSpec((tm, d), lambda i: (i, 0))
    full = lambda a: pl.BlockSpec(a.shape, lambda i: (0,) * a.ndim)
    return pl.pallas_call(
        _final_kernel,
        grid=(n_lat // tm,),
        in_specs=[row, row, full(gmod), full(fw)],
        out_specs=row,
        out_shape=jax.ShapeDtypeStruct((n_lat, d), F32),
        compiler_params=_params(("parallel",)),
        name="final_norm",
    )(x, y, gmod, fw)


def _rope_tables(n_lat, n_ctx):
    axis_dim = DIFF_HEAD_DIM // 2
    quarter = axis_dim // 2
    rows = n_lat // GRID_W
    row = jnp.broadcast_to(jnp.arange(rows, dtype=F32)[:, None], (rows, GRID_W)).reshape(-1)
    col = jnp.broadcast_to(jnp.arange(GRID_W, dtype=F32)[None, :], (rows, GRID_W)).reshape(-1)
    inv_freq = ROPE_THETA ** (-jnp.arange(0, axis_dim, 2, dtype=F32) / axis_dim)
    ang_r = row[:, None] * inv_freq
    ang_c = col[:, None] * inv_freq
    zeros = jnp.zeros((n_lat, quarter), F32)
    c = jnp.concatenate([jnp.cos(ang_r), jnp.cos(ang_r), jnp.cos(ang_c), jnp.cos(ang_c)], axis=1)
    sa = jnp.concatenate([-jnp.sin(ang_r), zeros, -jnp.sin(ang_c), zeros], axis=1)
    sb = jnp.concatenate([zeros, jnp.sin(ang_r), zeros, jnp.sin(ang_c)], axis=1)
    pad = lambda a, v: jnp.concatenate([a, jnp.full((n_ctx, DIFF_HEAD_DIM), v, F32)], axis=0)
    return pad(c, 1.0), pad(sa, 0.0), pad(sb, 0.0)


def _pack_mod(m, d):
    z = jnp.zeros((MOD_CTX_ROW - N_MOD, d), F32)
    return jnp.concatenate([m[0].reshape(N_MOD, d), z, m[1].reshape(N_MOD, d), z], axis=0)


def kernel(x, c, ctx, c_ctx, ada_w, ada_b, norm_w, pool_w, pool_b, pool_scale, attn_w_qkv, attn_w_o,
           attn_lambda, attn_subln_w, router_group_w, router_expert_w, expert_w1, expert_w3, expert_w2,
           final_norm_w):
    _, n_lat, d = x.shape
    n_ctx = ctx.shape[1]
    depth = ada_w.shape[0]
    assert x.shape[0] == 1 and n_lat % n_ctx == 0 and n_ctx % 256 == 0 and d % (4 * LANES) == 0

    mods = _adaln(c, c_ctx, ada_w, ada_b)
    mod = [_pack_mod(mods[i], d) for i in range(depth)]

    wr = jnp.concatenate([router_group_w, router_expert_w], axis=-1)
    wr = jnp.pad(wr, ((0, 0), (0, 0), (0, LANES - wr.shape[-1])))
    wr_hi = wr.astype(BF16)
    wr_lo = (wr - wr_hi.astype(F32)).astype(BF16)
    wr2 = jnp.concatenate([wr_hi, wr_lo], axis=-1)

    pool_w16 = pool_w.astype(BF16)
    wqkv16 = attn_w_qkv.astype(BF16)
    wo16 = attn_w_o.astype(BF16)
    w13_16 = jnp.concatenate([expert_w1, expert_w3], axis=-1).astype(BF16)
    w2_16 = expert_w2.astype(BF16)
    rope_c, rope_sa, rope_sb = _rope_tables(n_lat, n_ctx)

    xs, y = x[0], None
    for i in range(depth):
        jm = i // 2
        gprev = mod[i - 1] if i > 0 else None
        if i % 2 == 0:
            xs, h2, gates = _pool_call(xs, y, gprev, mod[i], norm_w[i], pool_w16, pool_b[jm],
                                       pool_scale[jm].reshape(1, d), wr2, n_lat=n_lat, layer=i, mixer=jm,
                                       ctx=ctx[0] if i == 0 else None)
        else:
            lambda_init = 0.8 - 0.6 * math.exp(-0.3 * i)
            xs, h, _ = _norm_call(xs, y, gprev, mod[i], norm_w[i, 0:1], None, k_shift=0, n_lat=n_lat)
            qk, vt = _qkv_call(h, wqkv16, jm, rope_c, rope_sa, rope_sb)
            o = _attn_call(qk, vt, attn_lambda[jm], attn_subln_w[jm], lambda_init=lambda_init,
                           n_lat=n_lat, n_ctx=n_ctx)
            xs = _oproj_call(o, wo16, jm, xs, mod[i], n_lat=n_lat)
            _, h2, gates = _norm_call(xs, None, None, mod[i], norm_w[i, 1:2], wr2, k_shift=3, n_lat=n_lat,
                                      layer=i)
        y = _moe_call(h2, gates, w13_16, w2_16, i)
    out = _final_call(xs, y, mod[depth - 1], final_norm_w.reshape(1, d), n_lat=n_lat)
    return out[None]
```

```python
import functools
import math

import jax
import jax.numpy as jnp
from jax import lax
from jax.experimental import pallas as pl
from jax.experimental.pallas import tpu as pltpu

F32 = jnp.float32
BF16 = jnp.bfloat16

GRID_W = 64
RMS_EPS = 1e-6
ROPE_THETA = 10000.0
POOL_WINDOWS = (2, 4, 8, 16)
N_MOD = 6
N_EXPERT_GROUPS = 4
EXPERTS_PER_GROUP = 4
N_EXPERTS = N_EXPERT_GROUPS * EXPERTS_PER_GROUP
DIFF_HEAD_DIM = 128
HEAD_W = 2 * DIFF_HEAD_DIM
LANES = 128
SUBLANES = 8
MXU_DIM = 256
MOD_CTX_ROW = 8
ROUTER_COL0 = N_EXPERT_GROUPS
PAIRS_PER_GROUP = 6
MOE_TILE = 128
ADALN_COLS = 2048
VMEM_LIMIT = 56 * 1024 * 1024
NEG_BIG = -1e30
LOG2E = 1.4426950408889634


def _pick(n, candidates):
    for c in candidates:
        if c <= n and n % c == 0:
            return c
    return n


def _params(semantics):
    return pltpu.CompilerParams(dimension_semantics=semantics, vmem_limit_bytes=VMEM_LIMIT)


def _layer_spec(stacked, layer):
    return pl.BlockSpec((None,) + stacked.shape[1:], lambda *_: (layer,) + (0,) * (stacked.ndim - 1))


def _silu(v):
    return v * jax.nn.sigmoid(v)


def _adaln_kernel(c_ref, cc_ref, w_ref, b_ref, o_ref, acc_ref, s_ref, *, tk, tn, nk):
    k = pl.program_id(2)

    @pl.when(k == 0)
    def _():
        acc_ref[...] = jnp.zeros_like(acc_ref)

    s_ref[0] = _silu(c_ref[...])
    s_ref[1] = _silu(cc_ref[...])

    cw = min(tn, ADALN_COLS)
    reps = cw // LANES
    for c0 in range(0, tn, cw):
        def body(j, carry, c0=c0):
            a0, a1 = carry
            r = pl.multiple_of(j * SUBLANES, SUBLANES)
            w = w_ref[pl.ds(r, SUBLANES), c0:c0 + cw]
            a0 = a0 + w * jnp.concatenate([s_ref[0, pl.ds(r, SUBLANES), :]] * reps, axis=1)
            a1 = a1 + w * jnp.concatenate([s_ref[1, pl.ds(r, SUBLANES), :]] * reps, axis=1)
            return a0, a1

        z = jnp.zeros((SUBLANES, cw), F32)
        a0, a1 = lax.fori_loop(0, tk // SUBLANES, body, (z, z), unroll=4)
        acc_ref[0, :, c0:c0 + cw] += a0
        acc_ref[1, :, c0:c0 + cw] += a1

    @pl.when(k == nk - 1)
    def _():
        o_ref[0:1, :] = jnp.sum(acc_ref[0], axis=0, keepdims=True) + b_ref[...]
        o_ref[1:2, :] = jnp.sum(acc_ref[1], axis=0, keepdims=True) + b_ref[...]


def _adaln(c, c_ctx, ada_w, ada_b):
    depth, d, n = ada_w.shape
    tk = _pick(d, (512, 256, 128))
    tn = _pick(n, (4096, 2048, 1024, 512, 256, 128))
    nk = d // tk
    c_b = jnp.broadcast_to(c.reshape(d, 1), (d, LANES))
    cc_b = jnp.broadcast_to(c_ctx.reshape(d, 1), (d, LANES))
    return pl.pallas_call(
        functools.partial(_adaln_kernel, tk=tk, tn=tn, nk=nk),
        grid=(depth, n // tn, nk),
        in_specs=[
            pl.BlockSpec((tk, LANES), lambda l, j, k: (k, 0)),
            pl.BlockSpec((tk, LANES), lambda l, j, k: (k, 0)),
            pl.BlockSpec((None, tk, tn), lambda l, j, k: (l, k, j)),
            pl.BlockSpec((None, 1, tn), lambda l, j, k: (l, 0, j)),
        ],
        out_specs=pl.BlockSpec((None, 2, tn), lambda l, j, k: (l, 0, j)),
        out_shape=jax.ShapeDtypeStruct((depth, 2, n), F32),
        scratch_shapes=[pltpu.VMEM((2, SUBLANES, tn), F32), pltpu.VMEM((2, tk, LANES), F32)],
        compiler_params=_params(("parallel", "parallel", "arbitrary")),
        name="adaln",
    )(c_b, cc_b, ada_w, ada_b.reshape(depth, 1, n))


def _mod_row(mod_ref, k, is_ctx):
    return jnp.where(is_ctx, mod_ref[MOD_CTX_ROW + k:MOD_CTX_ROW + k + 1, :], mod_ref[k:k + 1, :])


def _norm_mod(xv, w_row, shift, scale):
    r = lax.rsqrt(jnp.mean(xv * xv, axis=-1, keepdims=True) + RMS_EPS)
    return (xv * r) * (w_row * (1.0 + scale)) + shift


def _router_info(h, h_hi, wr_ref):
    h_lo = (h - h_hi.astype(F32)).astype(BF16)
    r1 = jnp.dot(h_hi, wr_ref[...], preferred_element_type=F32)
    r2 = jnp.dot(h_lo, wr_ref[:, :LANES], preferred_element_type=F32)
    logits = r1[:, :LANES] + r1[:, LANES:] + r2
    col = lax.broadcasted_iota(jnp.int32, logits.shape, 1).astype(F32)
    far = float(4 * LANES)
    is_g = col < float(N_EXPERT_GROUPS)
    mg = jnp.max(jnp.where(is_g, logits, NEG_BIG), axis=1, keepdims=True)
    denom = jnp.sum(jnp.where(is_g, jnp.exp(logits - mg), 0.0), axis=1, keepdims=True)
    pg_top = 1.0 / denom
    g_sel = jnp.min(jnp.where(is_g & (logits == mg), col, far), axis=1, keepdims=True)
    e_lo = float(ROUTER_COL0) + float(EXPERTS_PER_GROUP) * g_sel
    in_grp = (col >= e_lo) & (col < e_lo + float(EXPERTS_PER_GROUP))
    v1 = jnp.max(jnp.where(in_grp, logits, NEG_BIG), axis=1, keepdims=True)
    i1 = jnp.min(jnp.where(in_grp & (logits == v1), col, far), axis=1, keepdims=True)
    rest = in_grp & (col != i1)
    v2 = jnp.max(jnp.where(rest, logits, NEG_BIG), axis=1, keepdims=True)
    i2 = jnp.min(jnp.where(rest & (logits == v2), col, far), axis=1, keepdims=True)
    e2 = jnp.exp(v2 - v1)
    w1 = pg_top / (1.0 + e2)
    w2 = pg_top * (e2 / (1.0 + e2))
    first_lower = i1 < i2
    ja = jnp.minimum(i1, i2) - e_lo
    jb = jnp.maximum(i1, i2) - e_lo
    bucket = g_sel * float(PAIRS_PER_GROUP) + ja * (7.0 - ja) * 0.5 + (jb - ja - 1.0)
    g_lower = jnp.where(first_lower, w1, w2)
    g_higher = jnp.where(first_lower, w2, w1)
    return jnp.where(col == 0.0, bucket, jnp.where(col == 1.0, g_lower, jnp.where(col == 2.0, g_higher, 0.0)))


def _emit_routed_rows(h, wr_ref, hx_ref, info_ref):
    d = h.shape[1]
    info = _router_info(h, h.astype(BF16), wr_ref)
    hx_ref[:, :d] = h
    hx_ref[:, d:] = info
    info_ref[...] = info


def _norm_kernel(*refs, has_y, router, k_shift, lat_tiles):
    it = iter(refs)
    x_ref = next(it)
    y_ref = next(it) if has_y else None
    gmod_ref = next(it) if has_y else None
    mod_ref = next(it)
    nw_ref = next(it)
    wr_ref = next(it) if router else None
    xo_ref = next(it) if has_y else None
    h_ref = next(it)
    g_ref = next(it) if router else None

    is_ctx = pl.program_id(0) >= lat_tiles
    xv = x_ref[...]
    if has_y:
        xv = xv + _mod_row(gmod_ref, 5, is_ctx) * y_ref[...]
        xo_ref[...] = xv
    h = _norm_mod(xv, nw_ref[...], _mod_row(mod_ref, k_shift, is_ctx), _mod_row(mod_ref, k_shift + 1, is_ctx))
    if router:
        _emit_routed_rows(h, wr_ref, h_ref, g_ref)
    else:
        h_ref[...] = h.astype(BF16)


def _norm_call(x, y, gmod, mod, nw, wr, *, k_shift, n_lat, layer=0):
    t, d = x.shape
    tm = 256
    has_y = y is not None
    router = wr is not None
    row = pl.BlockSpec((tm, d), lambda i: (i, 0))
    full = lambda a: pl.BlockSpec(a.shape, lambda i: (0,) * a.ndim)
    ins, specs = [x], [row]
    if has_y:
        ins += [y, gmod]
        specs += [row, full(gmod)]
    ins += [mod, nw]
    specs += [full(mod), full(nw)]
    if router:
        ins.append(wr)
        specs.append(_layer_spec(wr, layer))
    outs, ospecs = [], []
    if has_y:
        outs.append(jax.ShapeDtypeStruct((t, d), F32))
        ospecs.append(row)
    if router:
        outs += [jax.ShapeDtypeStruct((t, d + LANES), F32), jax.ShapeDtypeStruct((t, LANES), F32)]
        ospecs += [pl.BlockSpec((tm, d + LANES), lambda i: (i, 0)), pl.BlockSpec((tm, LANES), lambda i: (i, 0))]
    else:
        outs.append(jax.ShapeDtypeStruct((t, d), BF16))
        ospecs.append(row)
    res = pl.pallas_call(
        functools.partial(_norm_kernel, has_y=has_y, router=router, k_shift=k_shift, lat_tiles=n_lat // tm),
        grid=(t // tm,),
        in_specs=specs,
        out_specs=ospecs,
        out_shape=outs,
        compiler_params=_params(("parallel",)),
        name="norm_mod",
    )(*ins)
    res = list(res)
    xo = res.pop(0) if has_y else x
    h = res.pop(0)
    g = res.pop(0) if router else None
    return xo, h, g


HALO = 8


def _pool_kernel(*refs, has_y, split_in, tp, cg, lat_tiles, n_tiles):
    it = iter(refs)
    x_ref, xp_ref, xn_ref = next(it), next(it), next(it)
    if split_in:
        c_ref, cp_ref, cn_ref = next(it), next(it), next(it)
    if has_y:
        y_ref, yp_ref, yn_ref, gmod_ref = next(it), next(it), next(it), next(it)
    mod_ref, nw_ref, pw_ref, pb_ref, ps_ref, wr_ref = (next(it) for _ in range(6))
    x1_ref, h2_ref, g_ref = next(it), next(it), next(it)
    hext_ref, xin_ref = next(it), next(it)

    j = pl.program_id(0)
    is_ctx = j >= lat_tiles
    first = (j == 0) | (j == lat_tiles)
    last = (j == lat_tiles - 1) | (j == n_tiles - 1)

    nw = nw_ref[0:1, :]
    ws = nw * (1.0 + _mod_row(mod_ref, 1, is_ctx))
    shift = _mod_row(mod_ref, 0, is_ctx)

    def x_in(xr, yr, cr):
        xv = xr[...]
        if split_in:
            xv = jnp.where(is_ctx, cr[...], xv)
        if has_y:
            xv = xv + _mod_row(gmod_ref, 5, is_ctx) * yr[...]
        return xv

    def h_of(xv):
        r = lax.rsqrt(jnp.mean(xv * xv, axis=-1, keepdims=True) + RMS_EPS)
        return (xv * r) * ws + shift

    xm = x_in(x_ref, y_ref if has_y else None, c_ref if split_in else None)
    xin_ref[...] = xm
    hext_ref[HALO:HALO + tp, :] = h_of(xm)
    hp = h_of(x_in(xp_ref, yp_ref if has_y else None, cp_ref if split_in else None))
    hext_ref[0:HALO, :] = jnp.where(first, 0.0, hp)
    hn = h_of(x_in(xn_ref, yn_ref if has_y else None, cn_ref if split_in else None))
    hext_ref[HALO + tp:HALO + tp + HALO, :] = jnp.where(last, 0.0, hn)

    hext_ref[tp + 2 * HALO:, :] = jnp.zeros((hext_ref.shape[0] - tp - 2 * HALO, hext_ref.shape[1]), F32)

    tpos = lax.broadcasted_iota(jnp.int32, (tp, 1), 0).astype(F32)
    lo_lim = jnp.where(first, 0.0, -float(2 * HALO))
    hi_lim = jnp.where(last, float(tp), float(tp + 2 * HALO))
    gate = _mod_row(mod_ref, 2, is_ctx)
    win_t = lax.broadcasted_iota(jnp.int32, (tp, hext_ref.shape[0]), 0)
    win_s = lax.broadcasted_iota(jnp.int32, (tp, hext_ref.shape[0]), 1)
    for g, w in enumerate(POOL_WINDOWS):
        c0 = g * cg
        half = w // 2
        band = jnp.where((win_s >= win_t + (HALO - half)) & (win_s < win_t + (HALO + half)), 1.0, 0.0).astype(BF16)
        hx = hext_ref[:, c0:c0 + cg]
        hx_hi = hx.astype(BF16)
        hx_lo = (hx - hx_hi.astype(F32)).astype(BF16)
        wsum = (jnp.dot(band, hx_hi, preferred_element_type=F32)
                + jnp.dot(band, hx_lo, preferred_element_type=F32))
        cnt = jnp.minimum(tpos + float(half), hi_lim) - jnp.maximum(tpos - float(half), lo_lim)
        pooled = wsum / cnt - hext_ref[HALO:HALO + tp, c0:c0 + cg]
        yg = jnp.dot(pooled.astype(BF16), pw_ref[g], preferred_element_type=F32) + pb_ref[g:g + 1, :]
        yg = yg * ps_ref[:, c0:c0 + cg]
        x1_ref[:, c0:c0 + cg] = xin_ref[:, c0:c0 + cg] + gate[:, c0:c0 + cg] * yg

    x1 = x1_ref[...]
    h2 = _norm_mod(x1, nw_ref[1:2, :], _mod_row(mod_ref, 3, is_ctx), _mod_row(mod_ref, 4, is_ctx))
    _emit_routed_rows(h2, wr_ref, h2_ref, g_ref)


def _pool_call(x, y, gmod, mod, nw2, pw, pb, ps, wr, *, n_lat, layer, mixer, ctx=None):
    d = x.shape[1]
    split_in = ctx is not None
    t = x.shape[0] + (ctx.shape[0] if split_in else 0)
    tp = 128
    cg = d // len(POOL_WINDOWS)
    has_y = y is not None
    n_tiles = t // tp
    lat_tiles = n_lat // tp
    bpt = tp // HALO

    def halo_specs(n_rows, tile0):
        n_t, last_blk = n_rows // tp, n_rows // HALO - 1
        clip = lambda v, hi: jnp.minimum(jnp.maximum(v, 0), hi)
        return [pl.BlockSpec((tp, d), lambda i: (clip(i - tile0, n_t - 1), 0)),
                pl.BlockSpec((HALO, d), lambda i: (clip((i - tile0) * bpt - 1, last_blk), 0)),
                pl.BlockSpec((HALO, d), lambda i: (clip((i - tile0 + 1) * bpt, last_blk), 0))]

    row = pl.BlockSpec((tp, d), lambda i: (i, 0))
    full = lambda a: pl.BlockSpec(a.shape, lambda i: (0,) * a.ndim)
    ins, specs = [x, x, x], halo_specs(x.shape[0], 0)
    if split_in:
        ins += [ctx, ctx, ctx]
        specs += halo_specs(ctx.shape[0], lat_tiles)
    _, prev, nxt = halo_specs(t, 0)
    if has_y:
        ins += [y, y, y, gmod]
        specs += [row, prev, nxt, full(gmod)]
    ins += [mod, nw2, pw, pb, ps, wr]
    specs += [full(mod), full(nw2), _layer_spec(pw, mixer), full(pb), full(ps), _layer_spec(wr, layer)]
    return pl.pallas_call(
        functools.partial(_pool_kernel, has_y=has_y, split_in=split_in, tp=tp, cg=cg, lat_tiles=lat_tiles,
                          n_tiles=n_tiles),
        grid=(n_tiles,),
        in_specs=specs,
        out_specs=[row, pl.BlockSpec((tp, d + LANES), lambda i: (i, 0)), pl.BlockSpec((tp, LANES), lambda i: (i, 0))],
        out_shape=[jax.ShapeDtypeStruct((t, d), F32), jax.ShapeDtypeStruct((t, d + LANES), F32),
                   jax.ShapeDtypeStruct((t, LANES), F32)],
        scratch_shapes=[pltpu.VMEM((-(-(tp + 2 * HALO) // MXU_DIM) * MXU_DIM, d), F32), pltpu.VMEM((tp, d), F32)],
        compiler_params=_params(("parallel",)),
        name="pool_layer",
    )(*ins)


def _qk_kernel(a_ref, w_ref, c_ref, sa_ref, sb_ref, o_ref, *, d, tn, qscale):
    acc = jnp.dot(a_ref[...], w_ref[...], preferred_element_type=F32)
    f = jnp.where(pl.program_id(1) < d // tn, qscale, 1.0)
    cc = c_ref[...] * f
    sa = sa_ref[...] * f
    sb = sb_ref[...] * f
    for b in range(tn // LANES):
        blk = acc[:, b * LANES:(b + 1) * LANES]
        rot = blk * cc + pltpu.roll(blk, 96, 1) * sa + pltpu.roll(blk, 32, 1) * sb
        o_ref[:, b * LANES:(b + 1) * LANES] = rot.astype(BF16)


def _vt_kernel(a_ref, w_ref, o_ref, acc_ref):
    acc_ref[...] = jnp.dot(a_ref[...], w_ref[...], preferred_element_type=F32)
    o_ref[...] = acc_ref[...].T.astype(BF16)


def _qkv_call(h, w, mixer, rope_c, rope_sa, rope_sb):
    t, d = h.shape
    tm = _pick(t, (640, 512, 256, 128))
    tn = _pick(d, (1024, 512, 256))
    qscale = (DIFF_HEAD_DIM ** -0.5) * LOG2E
    tab = pl.BlockSpec((tm, LANES), lambda i, j: (i, 0))
    a_spec = pl.BlockSpec((tm, d), lambda i, j: (i, 0))
    qk = pl.pallas_call(
        functools.partial(_qk_kernel, d=d, tn=tn, qscale=qscale),
        grid=(t // tm, (2 * d) // tn),
        in_specs=[a_spec, pl.BlockSpec((None, d, tn), lambda i, j: (mixer, 0, j)), tab, tab, tab],
        out_specs=pl.BlockSpec((tm, tn), lambda i, j: (i, j)),
        out_shape=jax.ShapeDtypeStruct((t, 2 * d), BF16),
        compiler_params=_params(("parallel", "arbitrary")),
        name="qk_proj",
    )(h, w, rope_c, rope_sa, rope_sb)
    v_col0 = (2 * d) // tn
    vt = pl.pallas_call(
        _vt_kernel,
        grid=(t // tm, d // tn),
        in_specs=[a_spec, pl.BlockSpec((None, d, tn), lambda i, j: (mixer, 0, v_col0 + j))],
        out_specs=pl.BlockSpec((tn, tm), lambda i, j: (j, i)),
        out_shape=jax.ShapeDtypeStruct((d, t), BF16),
        scratch_shapes=[pltpu.VMEM((tm, tn), F32)],
        compiler_params=_params(("parallel", "arbitrary")),
        name="v_proj_t",
    )(h, w)
    return qk, vt


def _oproj_kernel(a_ref, w_ref, x_ref, mod_ref, o_ref, *, tm, n_lat):
    acc = jnp.dot(a_ref[...], w_ref[...], preferred_element_type=F32)
    rows = pl.program_id(0) * tm + lax.broadcasted_iota(jnp.int32, (tm, 1), 0)
    gate = jnp.where(rows >= n_lat, mod_ref[MOD_CTX_ROW + 2:MOD_CTX_ROW + 3, :], mod_ref[2:3, :])
    o_ref[...] = x_ref[...] + gate * acc


def _oproj_call(o, w, mixer, x, mod, *, n_lat):
    t, d = x.shape
    tm = _pick(t, (640, 512, 256, 128))
    tn = _pick(d, (1024, 512, 256))
    return pl.pallas_call(
        functools.partial(_oproj_kernel, tm=tm, n_lat=n_lat),
        grid=(t // tm, d // tn),
        in_specs=[pl.BlockSpec((tm, d), lambda i, j: (i, 0)),
                  pl.BlockSpec((None, d, tn), lambda i, j: (mixer, 0, j)),
                  pl.BlockSpec((tm, tn), lambda i, j: (i, j)), pl.BlockSpec((16, tn), lambda i, j: (0, j))],
        out_specs=pl.BlockSpec((tm, tn), lambda i, j: (i, j)),
        out_shape=jax.ShapeDtypeStruct((t, d), F32),
        compiler_params=_params(("parallel", "arbitrary")),
        name="attn_out_proj",
    )(o, w, x, mod)


def _diff_lambda(lam_ref, lambda_init):
    lv = lam_ref[...]
    a = jnp.sum(lv[0:1, :] * lv[1:2, :], axis=1, keepdims=True)
    b = jnp.sum(lv[2:3, :] * lv[3:4, :], axis=1, keepdims=True)
    return jnp.exp(a) - jnp.exp(b) + lambda_init


def _attn_scores_t(q_ref, k_ref, c):
    qc = q_ref[:, c * DIFF_HEAD_DIM:(c + 1) * DIFF_HEAD_DIM]
    kc = k_ref[:, c * DIFF_HEAD_DIM:(c + 1) * DIFF_HEAD_DIM]
    return lax.dot_general(kc, qc, (((1,), (1,)), ((), ())), preferred_element_type=F32)


def _attn_finish(ot, sw_ref, lambda_init):
    r = lax.rsqrt(jnp.mean(ot * ot, axis=0, keepdims=True) + RMS_EPS)
    return (((ot * r).T * sw_ref[...]) * (1.0 - lambda_init)).astype(BF16)


def _attn_kernel(q_ref, k_ref, vt_ref, octx_ref, lam_ref, sw_ref, o_ref, m_ref, l_ref, acc_ref,
                 *, lambda_init, nq, nk, qp):
    qi = pl.program_id(1)
    ki = pl.program_id(2)

    @pl.when(qi < nq)
    def _():
        _attn_step(q_ref, k_ref, vt_ref, lam_ref, sw_ref, o_ref, m_ref, l_ref, acc_ref, ki,
                   lambda_init=lambda_init, nk=nk, qp=qp)

    @pl.when((qi == nq) & (ki == nk - 1))
    def _():
        n_ctx = octx_ref.shape[0]
        o_ref[0:n_ctx, :] = octx_ref[...]
        o_ref[n_ctx:, :] = jnp.zeros((o_ref.shape[0] - n_ctx, o_ref.shape[1]), o_ref.dtype)


def _attn_step(q_ref, k_ref, vt_ref, lam_ref, sw_ref, o_ref, m_ref, l_ref, acc_ref, ki, *, lambda_init, nk, qp):
    @pl.when(ki == 0)
    def _():
        m_ref[...] = jnp.full(m_ref.shape, NEG_BIG, F32)
        l_ref[...] = jnp.zeros_like(l_ref)
        acc_ref[...] = jnp.zeros_like(acc_ref)

    tq = q_ref.shape[0]
    chains = [(c, j) for j in range(tq // qp) for c in range(2)]

    def scores(c, j):
        qc = q_ref[j * qp:(j + 1) * qp, c * DIFF_HEAD_DIM:(c + 1) * DIFF_HEAD_DIM]
        kc = k_ref[:, c * DIFF_HEAD_DIM:(c + 1) * DIFF_HEAD_DIM]
        return lax.dot_general(kc, qc, (((1,), (1,)), ((), ())), preferred_element_type=F32)

    s_next = scores(*chains[0])
    for n, (c, j) in enumerate(chains):
        s = s_next
        if n + 1 < len(chains):
            s_next = scores(*chains[n + 1])
        cols = slice(j * qp, (j + 1) * qp)
        m_prev = m_ref[c, :, cols]
        m_new = jnp.maximum(m_prev, jnp.max(s, axis=0, keepdims=True))
        alpha = jnp.exp2(m_prev - m_new)
        p = jnp.exp2(s - m_new)
        l_ref[c, :, cols] = alpha * l_ref[c, :, cols] + jnp.sum(p, axis=0, keepdims=True)
        acc_ref[c, :, cols] = alpha * acc_ref[c, :, cols] + jnp.dot(
            vt_ref[...], p.astype(BF16), preferred_element_type=F32)
        m_ref[c, :, cols] = m_new

    @pl.when(ki == nk - 1)
    def _():
        lam = _diff_lambda(lam_ref, lambda_init)
        ot = acc_ref[0] / l_ref[0] - lam * (acc_ref[1] / l_ref[1])
        o_ref[...] = _attn_finish(ot, sw_ref, lambda_init)


def _attn_ctx_kernel(q_ref, k_ref, vt_ref, lam_ref, sw_ref, o_ref, *, lambda_init):
    lam = _diff_lambda(lam_ref, lambda_init)
    outs = []
    for c in range(2):
        s = _attn_scores_t(q_ref, k_ref, c)
        p = jnp.exp2(s - jnp.max(s, axis=0, keepdims=True))
        l = jnp.sum(p, axis=0, keepdims=True)
        outs.append(jnp.dot(vt_ref[...], p.astype(BF16), preferred_element_type=F32) / l)
    o_ref[...] = _attn_finish(outs[0] - lam * outs[1], sw_ref, lambda_init)


def _attn_call(qk, vt, lam_vecs, subln_w, *, lambda_init, n_lat, n_ctx):
    d, t = vt.shape
    heads = d // HEAD_W
    tq = _pick(n_lat, (2048, 1024, 512, 256, 128))
    tk = _pick(t, (3328, 1280, 640, 256, 128))
    nq = n_lat // tq
    nk = t // tk
    cblk = n_lat // n_ctx
    assert n_ctx <= tq
    sw = subln_w.reshape(1, HEAD_W)
    small = lambda a: pl.BlockSpec(a.shape, lambda *_: (0,) * a.ndim)
    o_ctx = pl.pallas_call(
        functools.partial(_attn_ctx_kernel, lambda_init=lambda_init),
        grid=(heads,),
        in_specs=[
            pl.BlockSpec((n_ctx, HEAD_W), lambda h: (cblk, h)),
            pl.BlockSpec((n_ctx, HEAD_W), lambda h: (cblk, heads + h)),
            pl.BlockSpec((HEAD_W, n_ctx), lambda h: (h, cblk)),
            small(lam_vecs), small(sw),
        ],
        out_specs=pl.BlockSpec((n_ctx, HEAD_W), lambda h: (0, h)),
        out_shape=jax.ShapeDtypeStruct((n_ctx, d), BF16),
        compiler_params=_params(("parallel",)),
        name="diff_attn_ctx",
    )(qk, qk, vt, lam_vecs, sw)
    last_q = nq - 1
    kv_blk = lambda i, k: jnp.where(i < nq, k, nk - 1)
    return pl.pallas_call(
        functools.partial(_attn_kernel, lambda_init=lambda_init, nq=nq, nk=nk, qp=min(tq, 256)),
        grid=(heads, nq + 1, nk),
        in_specs=[
            pl.BlockSpec((tq, HEAD_W), lambda h, i, k: (jnp.minimum(i, last_q), h)),
            pl.BlockSpec((tk, HEAD_W), lambda h, i, k: (kv_blk(i, k), heads + h)),
            pl.BlockSpec((HEAD_W, tk), lambda h, i, k: (h, kv_blk(i, k))),
            pl.BlockSpec((n_ctx, HEAD_W), lambda h, i, k: (0, h)),
            small(lam_vecs), small(sw),
        ],
        out_specs=pl.BlockSpec((tq, HEAD_W), lambda h, i, k: (i, h)),
        out_shape=jax.ShapeDtypeStruct(((nq + 1) * tq, d), BF16),
        scratch_shapes=[pltpu.VMEM((2, 1, tq), F32), pltpu.VMEM((2, 1, tq), F32),
                        pltpu.VMEM((2, HEAD_W, tq), F32)],
        compiler_params=_params(("parallel", "parallel", "arbitrary")),
        name="diff_attn",
    )(qk, qk, vt, o_ctx, lam_vecs, sw)


def _moe_kernel(elo_ref, ehi_ref, src_ref, srcn_ref, dst_ref, x_hbm, w13l_ref, w13h_ref, w2l_ref, w2h_ref,
                y_hbm, xbuf, ybuf, gsem, ssem, *, ts, d, n_tiles):
    del elo_ref, ehi_ref
    i = pl.program_id(0)
    slot = i % 2
    nslot = 1 - slot

    def start_gather(idx_ref, sl):
        for r in range(ts):
            pltpu.make_async_copy(x_hbm.at[pl.ds(idx_ref[0, r], 1)], xbuf.at[sl, pl.ds(r, 1)],
                                  gsem.at[sl]).start(priority=r % 2)

    def wait_gather(sl):
        pltpu.make_async_copy(x_hbm.at[pl.ds(0, ts)], xbuf.at[sl], gsem.at[sl]).wait()

    def wait_scatter(sl):
        pltpu.make_async_copy(ybuf.at[sl], y_hbm.at[pl.ds(0, ts)], ssem.at[sl]).wait()

    @pl.when(i == 0)
    def _():
        start_gather(src_ref, 0)

    wait_gather(slot)

    @pl.when(i >= 2)
    def _():
        wait_scatter(slot)

    start_gather(srcn_ref, nslot)

    xin = xbuf[slot]
    hv = xin[:, :d].astype(BF16)
    g_lower = xin[:, d + 1:d + 2]
    g_higher = xin[:, d + 2:d + 3]
    f = w2l_ref.shape[0]

    def hidden(w13_ref, gate):
        au = jnp.dot(hv, w13_ref[...], preferred_element_type=F32)
        return ((_silu(au[:, :f]) * au[:, f:]) * gate).astype(BF16)

    ybuf[slot] = (jnp.dot(hidden(w13l_ref, g_lower), w2l_ref[...], preferred_element_type=F32)
                  + jnp.dot(hidden(w13h_ref, g_higher), w2h_ref[...], preferred_element_type=F32))

    for r in range(ts):
        pltpu.make_async_copy(ybuf.at[slot, pl.ds(r, 1)], y_hbm.at[pl.ds(dst_ref[0, r], 1)],
                              ssem.at[slot]).start(priority=r % 2)

    @pl.when(i == n_tiles - 1)
    def _():
        wait_scatter(slot)
        wait_scatter(nslot)
        wait_gather(nslot)


def _w13_kernel(w1_ref, w3_ref, o_ref):
    f = w1_ref.shape[-1]
    o_ref[:, :f] = w1_ref[...].astype(BF16)
    o_ref[:, f:] = w3_ref[...].astype(BF16)


def _w13_call(w1, w3):
    depth, n_e, d, f = w1.shape
    tr = _pick(d, (2048, 1024, 512, 256))
    spec_in = pl.BlockSpec((None, None, tr, f), lambda l, e, r: (l, e, r, 0))
    return pl.pallas_call(
        _w13_kernel,
        grid=(depth, n_e, d // tr),
        in_specs=[spec_in, spec_in],
        out_specs=pl.BlockSpec((None, None, tr, 2 * f), lambda l, e, r: (l, e, r, 0)),
        out_shape=jax.ShapeDtypeStruct((depth, n_e, d, 2 * f), BF16),
        compiler_params=_params(("parallel", "parallel", "parallel")),
        name="w13_pack",
    )(w1, w3)


def _moe_plan(info, *, ts, n_pad_rows):
    t = info.shape[0]
    n_buckets = N_EXPERT_GROUPS * PAIRS_PER_GROUP
    bucket = info[:, 0].astype(jnp.int32)
    onehot = (bucket[:, None] == jnp.arange(n_buckets, dtype=jnp.int32)[None, :]).astype(jnp.int32)
    csum = jnp.cumsum(onehot, axis=0)
    rank = jnp.sum(csum * onehot, axis=1) - 1
    counts = csum[-1]
    padded = ((counts + ts - 1) // ts) * ts
    ends = jnp.cumsum(padded)
    starts = ends - padded
    pos = jnp.sum(starts[None, :] * onehot, axis=1) + rank
    tok = jnp.arange(t, dtype=jnp.int32)
    slot_tok = jnp.full((n_pad_rows,), -1, jnp.int32).at[pos].set(tok)
    valid = slot_tok >= 0
    src = jnp.maximum(slot_tok, 0)
    pad_rank = jnp.cumsum(1 - valid.astype(jnp.int32)) - 1
    dst = jnp.where(valid, slot_tok, t + pad_rank).astype(jnp.int32)
    n_tiles = n_pad_rows // ts
    tile_bucket = jnp.sum((jnp.arange(n_tiles, dtype=jnp.int32)[:, None] * ts >= ends[None, :]).astype(jnp.int32),
                          axis=1)
    tile_bucket = jnp.minimum(tile_bucket, n_buckets - 1)
    pair = tile_bucket % PAIRS_PER_GROUP
    pair_lo = jnp.array([0, 0, 0, 1, 1, 2], jnp.int32)[pair]
    pair_hi = jnp.array([1, 2, 3, 2, 3, 3], jnp.int32)[pair]
    base = (tile_bucket // PAIRS_PER_GROUP) * EXPERTS_PER_GROUP
    return (src.reshape(n_tiles, 1, ts), dst.reshape(n_tiles, 1, ts),
            (base + pair_lo).astype(jnp.int32), (base + pair_hi).astype(jnp.int32))


def _moe_call(hx, info, w13, w2, layer):
    t, dx = hx.shape
    d = dx - LANES
    f = w2.shape[2]
    ts = MOE_TILE
    n_pad_rows = t + N_EXPERT_GROUPS * PAIRS_PER_GROUP * ts
    n_tiles = n_pad_rows // ts
    src, dst, e_lo, e_hi = _moe_plan(info, ts=ts, n_pad_rows=n_pad_rows)
    smem_tile = lambda fn: pl.BlockSpec((None, 1, ts), fn, memory_space=pltpu.SMEM)
    w_in = lambda pick: pl.BlockSpec((None, None, d, 2 * f), lambda i, lo, hi: (layer, pick(lo, hi)[i], 0, 0))
    w_out = lambda pick: pl.BlockSpec((None, None, f, d), lambda i, lo, hi: (layer, pick(lo, hi)[i], 0, 0))
    lower = lambda lo, hi: lo
    higher = lambda lo, hi: hi
    grid_spec = pltpu.PrefetchScalarGridSpec(
        num_scalar_prefetch=2,
        grid=(n_tiles,),
        in_specs=[
            smem_tile(lambda i, lo, hi: (i, 0, 0)),
            smem_tile(lambda i, lo, hi: (jnp.minimum(i + 1, n_tiles - 1), 0, 0)),
            smem_tile(lambda i, lo, hi: (i, 0, 0)),
            pl.BlockSpec(memory_space=pl.ANY),
            w_in(lower), w_in(higher), w_out(lower), w_out(higher),
        ],
        out_specs=pl.BlockSpec(memory_space=pl.ANY),
        scratch_shapes=[pltpu.VMEM((2, ts, dx), F32), pltpu.VMEM((2, ts, d), F32),
                        pltpu.SemaphoreType.DMA((2,)), pltpu.SemaphoreType.DMA((2,))],
    )
    return pl.pallas_call(
        functools.partial(_moe_kernel, ts=ts, d=d, n_tiles=n_tiles),
        grid_spec=grid_spec,
        out_shape=jax.ShapeDtypeStruct((n_pad_rows, d), F32),
        compiler_params=_params(("arbitrary",)),
        name="moe_routed",
    )(e_lo, e_hi, src, src, dst, hx, w13, w13, w2, w2)


def _final_kernel(x_ref, y_ref, gmod_ref, fw_ref, o_ref):
    xv = x_ref[...] + gmod_ref[5:6, :] * y_ref[...]
    r = lax.rsqrt(jnp.mean(xv * xv, axis=-1, keepdims=True) + RMS_EPS)
    o_ref[...] = (xv * r) * fw_ref[...]


def _final_call(x, y, gmod, fw, *, n_lat):
    _, d = x.shape
    tm = 256
    row = pl.BlockSpec((tm, d), lambda i: (i, 0))
    full = lambda a: pl.BlockSpec(a.shape, lambda i: (0,) * a.ndim)
    return pl.pallas_call(
        _final_kernel,
        grid=(n_lat // tm,),
        in_specs=[row, row, full(gmod), full(fw)],
        out_specs=row,
        out_shape=jax.ShapeDtypeStruct((n_lat, d), F32),
        compiler_params=_params(("parallel",)),
        name="final_norm",
    )(x, y, gmod, fw)


def _rope_tables(n_lat, n_ctx):
    axis_dim = DIFF_HEAD_DIM // 2
    quarter = axis_dim // 2
    rows = n_lat // GRID_W
    row = jnp.broadcast_to(jnp.arange(rows, dtype=F32)[:, None], (rows, GRID_W)).reshape(-1)
    col = jnp.broadcast_to(jnp.arange(GRID_W, dtype=F32)[None, :], (rows, GRID_W)).reshape(-1)
    inv_freq = ROPE_THETA ** (-jnp.arange(0, axis_dim, 2, dtype=F32) / axis_dim)
    ang_r = row[:, None] * inv_freq
    ang_c = col[:, None] * inv_freq
    zeros = jnp.zeros((n_lat, quarter), F32)
    c = jnp.concatenate([jnp.cos(ang_r), jnp.cos(ang_r), jnp.cos(ang_c), jnp.cos(ang_c)], axis=1)
    sa = jnp.concatenate([-jnp.sin(ang_r), zeros, -jnp.sin(ang_c), zeros], axis=1)
    sb = jnp.concatenate([zeros, jnp.sin(ang_r), zeros, jnp.sin(ang_c)], axis=1)
    pad = lambda a, v: jnp.concatenate([a, jnp.full((n_ctx, DIFF_HEAD_DIM), v, F32)], axis=0)
    return pad(c, 1.0), pad(sa, 0.0), pad(sb, 0.0)


def _pack_mod(m, d):
    z = jnp.zeros((MOD_CTX_ROW - N_MOD, d), F32)
    return jnp.concatenate([m[0].reshape(N_MOD, d), z, m[1].reshape(N_MOD, d), z], axis=0)


def kernel(x, c, ctx, c_ctx, ada_w, ada_b, norm_w, pool_w, pool_b, pool_scale, attn_w_qkv, attn_w_o,
           attn_lambda, attn_subln_w, router_group_w, router_expert_w, expert_w1, expert_w3, expert_w2,
           final_norm_w):
    _, n_lat, d = x.shape
    n_ctx = ctx.shape[1]
    depth = ada_w.shape[0]
    assert x.shape[0] == 1 and n_lat % n_ctx == 0 and n_ctx % 256 == 0 and d % (4 * LANES) == 0

    mods = _adaln(c, c_ctx, ada_w, ada_b)
    mod = [_pack_mod(mods[i], d) for i in range(depth)]

    wr = jnp.concatenate([router_group_w, router_expert_w], axis=-1)
    wr = jnp.pad(wr, ((0, 0), (0, 0), (0, LANES - wr.shape[-1])))
    wr_hi = wr.astype(BF16)
    wr_lo = (wr - wr_hi.astype(F32)).astype(BF16)
    wr2 = jnp.concatenate([wr_hi, wr_lo], axis=-1)

    pool_w16 = pool_w.astype(BF16)
    wqkv16 = attn_w_qkv.astype(BF16)
    wo16 = attn_w_o.astype(BF16)
    w13_16 = _w13_call(expert_w1, expert_w3)
    w2_16 = expert_w2.astype(BF16)
    rope_c, rope_sa, rope_sb = _rope_tables(n_lat, n_ctx)

    xs, y = x[0], None
    for i in range(depth):
        jm = i // 2
        gprev = mod[i - 1] if i > 0 else None
        if i % 2 == 0:
            xs, h2, gates = _pool_call(xs, y, gprev, mod[i], norm_w[i], pool_w16, pool_b[jm],
                                       pool_scale[jm].reshape(1, d), wr2, n_lat=n_lat, layer=i, mixer=jm,
                                       ctx=ctx[0] if i == 0 else None)
        else:
            lambda_init = 0.8 - 0.6 * math.exp(-0.3 * i)
            xs, h, _ = _norm_call(xs, y, gprev, mod[i], norm_w[i, 0:1], None, k_shift=0, n_lat=n_lat)
            qk, vt = _qkv_call(h, wqkv16, jm, rope_c, rope_sa, rope_sb)
            o = _attn_call(qk, vt, attn_lambda[jm], attn_subln_w[jm], lambda_init=lambda_init,
                           n_lat=n_lat, n_ctx=n_ctx)
            xs = _oproj_call(o, wo16, jm, xs, mod[i], n_lat=n_lat)
            _, h2, gates = _norm_call(xs, None, None, mod[i], norm_w[i, 1:2], wr2, k_shift=3, n_lat=n_lat,
                                      layer=i)
        y = _moe_call(h2, gates, w13_16, w2_16, i)
    out = _final_call(xs, y, mod[depth - 1], final_norm_w.reshape(1, d), n_lat=n_lat)
    return out[None]
```

```python
import functools
import math

import jax
import jax.numpy as jnp
from jax import lax
from jax.experimental import pallas as pl
from jax.experimental.pallas import tpu as pltpu

F32 = jnp.float32
BF16 = jnp.bfloat16

GRID_W = 64
RMS_EPS = 1e-6
ROPE_THETA = 10000.0
POOL_WINDOWS = (2, 4, 8, 16)
N_MOD = 6
N_EXPERT_GROUPS = 4
EXPERTS_PER_GROUP = 4
N_EXPERTS = N_EXPERT_GROUPS * EXPERTS_PER_GROUP
DIFF_HEAD_DIM = 128
HEAD_W = 2 * DIFF_HEAD_DIM
LANES = 128
SUBLANES = 8
MXU_DIM = 256
MOD_CTX_ROW = 8
ROUTER_COL0 = N_EXPERT_GROUPS
PAIRS_PER_GROUP = 6
MOE_TILE = 128
ADALN_COLS = 2048
VMEM_LIMIT = 56 * 1024 * 1024
NEG_BIG = -1e30
LOG2E = 1.4426950408889634


def _pick(n, candidates):
    for c in candidates:
        if c <= n and n % c == 0:
            return c
    return n


def _params(semantics):
    return pltpu.CompilerParams(dimension_semantics=semantics, vmem_limit_bytes=VMEM_LIMIT)


def _layer_spec(stacked, layer):
    return pl.BlockSpec((None,) + stacked.shape[1:], lambda *_: (layer,) + (0,) * (stacked.ndim - 1))


def _silu(v):
    return v * jax.nn.sigmoid(v)


def _adaln_kernel(c_ref, cc_ref, w_ref, b_ref, o_ref, acc_ref, s_ref, *, tk, tn, nk):
    k = pl.program_id(2)

    @pl.when(k == 0)
    def _():
        acc_ref[...] = jnp.zeros_like(acc_ref)

    s_ref[0] = _silu(c_ref[...])
    s_ref[1] = _silu(cc_ref[...])

    cw = min(tn, ADALN_COLS)
    reps = cw // LANES
    for c0 in range(0, tn, cw):
        def body(j, carry, c0=c0):
            a0, a1 = carry
            r = pl.multiple_of(j * SUBLANES, SUBLANES)
            w = w_ref[pl.ds(r, SUBLANES), c0:c0 + cw]
            a0 = a0 + w * jnp.concatenate([s_ref[0, pl.ds(r, SUBLANES), :]] * reps, axis=1)
            a1 = a1 + w * jnp.concatenate([s_ref[1, pl.ds(r, SUBLANES), :]] * reps, axis=1)
            return a0, a1

        z = jnp.zeros((SUBLANES, cw), F32)
        a0, a1 = lax.fori_loop(0, tk // SUBLANES, body, (z, z), unroll=4)
        acc_ref[0, :, c0:c0 + cw] += a0
        acc_ref[1, :, c0:c0 + cw] += a1

    @pl.when(k == nk - 1)
    def _():
        o_ref[0:1, :] = jnp.sum(acc_ref[0], axis=0, keepdims=True) + b_ref[...]
        o_ref[1:2, :] = jnp.sum(acc_ref[1], axis=0, keepdims=True) + b_ref[...]


def _adaln(c, c_ctx, ada_w, ada_b):
    depth, d, n = ada_w.shape
    tk = _pick(d, (512, 256, 128))
    tn = _pick(n, (4096, 2048, 1024, 512, 256, 128))
    nk = d // tk
    c_b = jnp.broadcast_to(c.reshape(d, 1), (d, LANES))
    cc_b = jnp.broadcast_to(c_ctx.reshape(d, 1), (d, LANES))
    return pl.pallas_call(
        functools.partial(_adaln_kernel, tk=tk, tn=tn, nk=nk),
        grid=(depth, n // tn, nk),
        in_specs=[
            pl.BlockSpec((tk, LANES), lambda l, j, k: (k, 0)),
            pl.BlockSpec((tk, LANES), lambda l, j, k: (k, 0)),
            pl.BlockSpec((None, tk, tn), lambda l, j, k: (l, k, j)),
            pl.BlockSpec((None, 1, tn), lambda l, j, k: (l, 0, j)),
        ],
        out_specs=pl.BlockSpec((None, 2, tn), lambda l, j, k: (l, 0, j)),
        out_shape=jax.ShapeDtypeStruct((depth, 2, n), F32),
        scratch_shapes=[pltpu.VMEM((2, SUBLANES, tn), F32), pltpu.VMEM((2, tk, LANES), F32)],
        compiler_params=_params(("parallel", "parallel", "arbitrary")),
        name="adaln",
    )(c_b, cc_b, ada_w, ada_b.reshape(depth, 1, n))


def _mod_row(mod_ref, k, is_ctx):
    return jnp.where(is_ctx, mod_ref[MOD_CTX_ROW + k:MOD_CTX_ROW + k + 1, :], mod_ref[k:k + 1, :])


def _norm_mod(xv, w_row, shift, scale):
    r = lax.rsqrt(jnp.mean(xv * xv, axis=-1, keepdims=True) + RMS_EPS)
    return (xv * r) * (w_row * (1.0 + scale)) + shift


def _router_info(h, h_hi, wr_ref):
    h_lo = (h - h_hi.astype(F32)).astype(BF16)
    r1 = jnp.dot(h_hi, wr_ref[...], preferred_element_type=F32)
    r2 = jnp.dot(h_lo, wr_ref[:, :LANES], preferred_element_type=F32)
    logits = r1[:, :LANES] + r1[:, LANES:] + r2
    col = lax.broadcasted_iota(jnp.int32, logits.shape, 1).astype(F32)
    far = float(4 * LANES)
    is_g = col < float(N_EXPERT_GROUPS)
    mg = jnp.max(jnp.where(is_g, logits, NEG_BIG), axis=1, keepdims=True)
    denom = jnp.sum(jnp.where(is_g, jnp.exp(logits - mg), 0.0), axis=1, keepdims=True)
    pg_top = 1.0 / denom
    g_sel = jnp.min(jnp.where(is_g & (logits == mg), col, far), axis=1, keepdims=True)
    e_lo = float(ROUTER_COL0) + float(EXPERTS_PER_GROUP) * g_sel
    in_grp = (col >= e_lo) & (col < e_lo + float(EXPERTS_PER_GROUP))
    v1 = jnp.max(jnp.where(in_grp, logits, NEG_BIG), axis=1, keepdims=True)
    i1 = jnp.min(jnp.where(in_grp & (logits == v1), col, far), axis=1, keepdims=True)
    rest = in_grp & (col != i1)
    v2 = jnp.max(jnp.where(rest, logits, NEG_BIG), axis=1, keepdims=True)
    i2 = jnp.min(jnp.where(rest & (logits == v2), col, far), axis=1, keepdims=True)
    e2 = jnp.exp(v2 - v1)
    w1 = pg_top / (1.0 + e2)
    w2 = pg_top * (e2 / (1.0 + e2))
    first_lower = i1 < i2
    ja = jnp.minimum(i1, i2) - e_lo
    jb = jnp.maximum(i1, i2) - e_lo
    bucket = g_sel * float(PAIRS_PER_GROUP) + ja * (7.0 - ja) * 0.5 + (jb - ja - 1.0)
    g_lower = jnp.where(first_lower, w1, w2)
    g_higher = jnp.where(first_lower, w2, w1)
    return jnp.where(col == 0.0, bucket, jnp.where(col == 1.0, g_lower, jnp.where(col == 2.0, g_higher, 0.0)))


def _emit_routed_rows(h, wr_ref, hx_ref, info_ref):
    d = h.shape[1]
    info = _router_info(h, h.astype(BF16), wr_ref)
    hx_ref[:, :d] = h
    hx_ref[:, d:] = info
    info_ref[...] = info


def _norm_kernel(*refs, has_y, router, k_shift, lat_tiles):
    it = iter(refs)
    x_ref = next(it)
    y_ref = next(it) if has_y else None
    gmod_ref = next(it) if has_y else None
    mod_ref = next(it)
    nw_ref = next(it)
    wr_ref = next(it) if router else None
    xo_ref = next(it) if has_y else None
    h_ref = next(it)
    g_ref = next(it) if router else None

    is_ctx = pl.program_id(0) >= lat_tiles
    xv = x_ref[...]
    if has_y:
        xv = xv + _mod_row(gmod_ref, 5, is_ctx) * y_ref[...]
        xo_ref[...] = xv
    h = _norm_mod(xv, nw_ref[...], _mod_row(mod_ref, k_shift, is_ctx), _mod_row(mod_ref, k_shift + 1, is_ctx))
    if router:
        _emit_routed_rows(h, wr_ref, h_ref, g_ref)
    else:
        h_ref[...] = h.astype(BF16)


def _norm_call(x, y, gmod, mod, nw, wr, *, k_shift, n_lat, layer=0):
    t, d = x.shape
    tm = 256
    has_y = y is not None
    router = wr is not None
    row = pl.BlockSpec((tm, d), lambda i: (i, 0))
    full = lambda a: pl.BlockSpec(a.shape, lambda i: (0,) * a.ndim)
    ins, specs = [x], [row]
    if has_y:
        ins += [y, gmod]
        specs += [row, full(gmod)]
    ins += [mod, nw]
    specs += [full(mod), full(nw)]
    if router:
        ins.append(wr)
        specs.append(_layer_spec(wr, layer))
    outs, ospecs = [], []
    if has_y:
        outs.append(jax.ShapeDtypeStruct((t, d), F32))
        ospecs.append(row)
    if router:
        outs += [jax.ShapeDtypeStruct((t, d + LANES), F32), jax.ShapeDtypeStruct((t, LANES), F32)]
        ospecs += [pl.BlockSpec((tm, d + LANES), lambda i: (i, 0)), pl.BlockSpec((tm, LANES), lambda i: (i, 0))]
    else:
        outs.append(jax.ShapeDtypeStruct((t, d), BF16))
        ospecs.append(row)
    res = pl.pallas_call(
        functools.partial(_norm_kernel, has_y=has_y, router=router, k_shift=k_shift, lat_tiles=n_lat // tm),
        grid=(t // tm,),
        in_specs=specs,
        out_specs=ospecs,
        out_shape=outs,
        compiler_params=_params(("parallel",)),
        name="norm_mod",
    )(*ins)
    res = list(res)
    xo = res.pop(0) if has_y else x
    h = res.pop(0)
    g = res.pop(0) if router else None
    return xo, h, g


HALO = 8


def _pool_kernel(*refs, has_y, split_in, tp, cg, lat_tiles, n_tiles):
    it = iter(refs)
    x_ref, xp_ref, xn_ref = next(it), next(it), next(it)
    if split_in:
        c_ref, cp_ref, cn_ref = next(it), next(it), next(it)
    if has_y:
        y_ref, yp_ref, yn_ref, gmod_ref = next(it), next(it), next(it), next(it)
    mod_ref, nw_ref, pw_ref, pb_ref, ps_ref, wr_ref = (next(it) for _ in range(6))
    x1_ref, h2_ref, g_ref = next(it), next(it), next(it)
    hext_ref, xin_ref = next(it), next(it)

    j = pl.program_id(0)
    is_ctx = j >= lat_tiles
    first = (j == 0) | (j == lat_tiles)
    last = (j == lat_tiles - 1) | (j == n_tiles - 1)

    nw = nw_ref[0:1, :]
    ws = nw * (1.0 + _mod_row(mod_ref, 1, is_ctx))
    shift = _mod_row(mod_ref, 0, is_ctx)

    def x_in(xr, yr, cr):
        xv = xr[...]
        if split_in:
            xv = jnp.where(is_ctx, cr[...], xv)
        if has_y:
            xv = xv + _mod_row(gmod_ref, 5, is_ctx) * yr[...]
        return xv

    def h_of(xv):
        r = lax.rsqrt(jnp.mean(xv * xv, axis=-1, keepdims=True) + RMS_EPS)
        return (xv * r) * ws + shift

    xm = x_in(x_ref, y_ref if has_y else None, c_ref if split_in else None)
    xin_ref[...] = xm
    hext_ref[HALO:HALO + tp, :] = h_of(xm)
    hp = h_of(x_in(xp_ref, yp_ref if has_y else None, cp_ref if split_in else None))
    hext_ref[0:HALO, :] = jnp.where(first, 0.0, hp)
    hn = h_of(x_in(xn_ref, yn_ref if has_y else None, cn_ref if split_in else None))
    hext_ref[HALO + tp:HALO + tp + HALO, :] = jnp.where(last, 0.0, hn)

    hext_ref[tp + 2 * HALO:, :] = jnp.zeros((hext_ref.shape[0] - tp - 2 * HALO, hext_ref.shape[1]), F32)

    tpos = lax.broadcasted_iota(jnp.int32, (tp, 1), 0).astype(F32)
    lo_lim = jnp.where(first, 0.0, -float(2 * HALO))
    hi_lim = jnp.where(last, float(tp), float(tp + 2 * HALO))
    gate = _mod_row(mod_ref, 2, is_ctx)
    win_t = lax.broadcasted_iota(jnp.int32, (tp, hext_ref.shape[0]), 0)
    win_s = lax.broadcasted_iota(jnp.int32, (tp, hext_ref.shape[0]), 1)
    for g, w in enumerate(POOL_WINDOWS):
        c0 = g * cg
        half = w // 2
        band = jnp.where((win_s >= win_t + (HALO - half)) & (win_s < win_t + (HALO + half)), 1.0, 0.0).astype(BF16)
        hx = hext_ref[:, c0:c0 + cg]
        hx_hi = hx.astype(BF16)
        hx_lo = (hx - hx_hi.astype(F32)).astype(BF16)
        wsum = (jnp.dot(band, hx_hi, preferred_element_type=F32)
                + jnp.dot(band, hx_lo, preferred_element_type=F32))
        cnt = jnp.minimum(tpos + float(half), hi_lim) - jnp.maximum(tpos - float(half), lo_lim)
        pooled = wsum / cnt - hext_ref[HALO:HALO + tp, c0:c0 + cg]
        yg = jnp.dot(pooled.astype(BF16), pw_ref[g], preferred_element_type=F32) + pb_ref[g:g + 1, :]
        yg = yg * ps_ref[:, c0:c0 + cg]
        x1_ref[:, c0:c0 + cg] = xin_ref[:, c0:c0 + cg] + gate[:, c0:c0 + cg] * yg

    x1 = x1_ref[...]
    h2 = _norm_mod(x1, nw_ref[1:2, :], _mod_row(mod_ref, 3, is_ctx), _mod_row(mod_ref, 4, is_ctx))
    _emit_routed_rows(h2, wr_ref, h2_ref, g_ref)


def _pool_call(x, y, gmod, mod, nw2, pw, pb, ps, wr, *, n_lat, layer, mixer, ctx=None):
    d = x.shape[1]
    split_in = ctx is not None
    t = x.shape[0] + (ctx.shape[0] if split_in else 0)
    tp = 128
    cg = d // len(POOL_WINDOWS)
    has_y = y is not None
    n_tiles = t // tp
    lat_tiles = n_lat // tp
    bpt = tp // HALO

    def halo_specs(n_rows, tile0):
        n_t, last_blk = n_rows // tp, n_rows // HALO - 1
        clip = lambda v, hi: jnp.minimum(jnp.maximum(v, 0), hi)
        return [pl.BlockSpec((tp, d), lambda i: (clip(i - tile0, n_t - 1), 0)),
                pl.BlockSpec((HALO, d), lambda i: (clip((i - tile0) * bpt - 1, last_blk), 0)),
                pl.BlockSpec((HALO, d), lambda i: (clip((i - tile0 + 1) * bpt, last_blk), 0))]

    row = pl.BlockSpec((tp, d), lambda i: (i, 0))
    full = lambda a: pl.BlockSpec(a.shape, lambda i: (0,) * a.ndim)
    ins, specs = [x, x, x], halo_specs(x.shape[0], 0)
    if split_in:
        ins += [ctx, ctx, ctx]
        specs += halo_specs(ctx.shape[0], lat_tiles)
    _, prev, nxt = halo_specs(t, 0)
    if has_y:
        ins += [y, y, y, gmod]
        specs += [row, prev, nxt, full(gmod)]
    ins += [mod, nw2, pw, pb, ps, wr]
    specs += [full(mod), full(nw2), _layer_spec(pw, mixer), full(pb), full(ps), _layer_spec(wr, layer)]
    return pl.pallas_call(
        functools.partial(_pool_kernel, has_y=has_y, split_in=split_in, tp=tp, cg=cg, lat_tiles=lat_tiles,
                          n_tiles=n_tiles),
        grid=(n_tiles,),
        in_specs=specs,
        out_specs=[row, pl.BlockSpec((tp, d + LANES), lambda i: (i, 0)), pl.BlockSpec((tp, LANES), lambda i: (i, 0))],
        out_shape=[jax.ShapeDtypeStruct((t, d), F32), jax.ShapeDtypeStruct((t, d + LANES), F32),
                   jax.ShapeDtypeStruct((t, LANES), F32)],
        scratch_shapes=[pltpu.VMEM((-(-(tp + 2 * HALO) // MXU_DIM) * MXU_DIM, d), F32), pltpu.VMEM((tp, d), F32)],
        compiler_params=_params(("parallel",)),
        name="pool_layer",
    )(*ins)


def _qk_kernel(a_ref, w_ref, c_ref, sa_ref, sb_ref, o_ref, *, d, tn, qscale):
    acc = jnp.dot(a_ref[...], w_ref[...], preferred_element_type=F32)
    f = jnp.where(pl.program_id(1) < d // tn, qscale, 1.0)
    cc = c_ref[...] * f
    sa = sa_ref[...] * f
    sb = sb_ref[...] * f
    for b in range(tn // LANES):
        blk = acc[:, b * LANES:(b + 1) * LANES]
        rot = blk * cc + pltpu.roll(blk, 96, 1) * sa + pltpu.roll(blk, 32, 1) * sb
        o_ref[:, b * LANES:(b + 1) * LANES] = rot.astype(BF16)


def _vt_kernel(a_ref, w_ref, o_ref, acc_ref):
    acc_ref[...] = jnp.dot(a_ref[...], w_ref[...], preferred_element_type=F32)
    o_ref[...] = acc_ref[...].T.astype(BF16)


def _qkv_call(h, w, mixer, rope_c, rope_sa, rope_sb):
    t, d = h.shape
    tm = _pick(t, (640, 512, 256, 128))
    tn = _pick(d, (1024, 512, 256))
    qscale = (DIFF_HEAD_DIM ** -0.5) * LOG2E
    tab = pl.BlockSpec((tm, LANES), lambda i, j: (i, 0))
    a_spec = pl.BlockSpec((tm, d), lambda i, j: (i, 0))
    qk = pl.pallas_call(
        functools.partial(_qk_kernel, d=d, tn=tn, qscale=qscale),
        grid=(t // tm, (2 * d) // tn),
        in_specs=[a_spec, pl.BlockSpec((None, d, tn), lambda i, j: (mixer, 0, j)), tab, tab, tab],
        out_specs=pl.BlockSpec((tm, tn), lambda i, j: (i, j)),
        out_shape=jax.ShapeDtypeStruct((t, 2 * d), BF16),
        compiler_params=_params(("parallel", "arbitrary")),
        name="qk_proj",
    )(h, w, rope_c, rope_sa, rope_sb)
    v_col0 = (2 * d) // tn
    vt = pl.pallas_call(
        _vt_kernel,
        grid=(t // tm, d // tn),
        in_specs=[a_spec, pl.BlockSpec((None, d, tn), lambda i, j: (mixer, 0, v_col0 + j))],
        out_specs=pl.BlockSpec((tn, tm), lambda i, j: (j, i)),
        out_shape=jax.ShapeDtypeStruct((d, t), BF16),
        scratch_shapes=[pltpu.VMEM((tm, tn), F32)],
        compiler_params=_params(("parallel", "arbitrary")),
        name="v_proj_t",
    )(h, w)
    return qk, vt


def _oproj_kernel(a_ref, w_ref, x_ref, mod_ref, o_ref, *, tm, n_lat):
    acc = jnp.dot(a_ref[...], w_ref[...], preferred_element_type=F32)
    rows = pl.program_id(0) * tm + lax.broadcasted_iota(jnp.int32, (tm, 1), 0)
    gate = jnp.where(rows >= n_lat, mod_ref[MOD_CTX_ROW + 2:MOD_CTX_ROW + 3, :], mod_ref[2:3, :])
    o_ref[...] = x_ref[...] + gate * acc


def _oproj_call(o, w, mixer, x, mod, *, n_lat):
    t, d = x.shape
    tm = _pick(t, (640, 512, 256, 128))
    tn = _pick(d, (1024, 512, 256))
    return pl.pallas_call(
        functools.partial(_oproj_kernel, tm=tm, n_lat=n_lat),
        grid=(t // tm, d // tn),
        in_specs=[pl.BlockSpec((tm, d), lambda i, j: (i, 0)),
                  pl.BlockSpec((None, d, tn), lambda i, j: (mixer, 0, j)),
                  pl.BlockSpec((tm, tn), lambda i, j: (i, j)), pl.BlockSpec((16, tn), lambda i, j: (0, j))],
        out_specs=pl.BlockSpec((tm, tn), lambda i, j: (i, j)),
        out_shape=jax.ShapeDtypeStruct((t, d), F32),
        compiler_params=_params(("parallel", "arbitrary")),
        name="attn_out_proj",
    )(o, w, x, mod)


def _diff_lambda(lam_ref, lambda_init):
    lv = lam_ref[...]
    a = jnp.sum(lv[0:1, :] * lv[1:2, :], axis=1, keepdims=True)
    b = jnp.sum(lv[2:3, :] * lv[3:4, :], axis=1, keepdims=True)
    return jnp.exp(a) - jnp.exp(b) + lambda_init


def _attn_scores_t(q_ref, k_ref, c):
    qc = q_ref[:, c * DIFF_HEAD_DIM:(c + 1) * DIFF_HEAD_DIM]
    kc = k_ref[:, c * DIFF_HEAD_DIM:(c + 1) * DIFF_HEAD_DIM]
    return lax.dot_general(kc, qc, (((1,), (1,)), ((), ())), preferred_element_type=F32)


def _attn_finish(ot, sw_ref, lambda_init):
    r = lax.rsqrt(jnp.mean(ot * ot, axis=0, keepdims=True) + RMS_EPS)
    return (((ot * r).T * sw_ref[...]) * (1.0 - lambda_init)).astype(BF16)


def _attn_kernel(q_ref, k_ref, vt_ref, octx_ref, lam_ref, sw_ref, o_ref, m_ref, l_ref, acc_ref,
                 *, lambda_init, nq, nk, qp):
    qi = pl.program_id(1)
    ki = pl.program_id(2)

    @pl.when(qi < nq)
    def _():
        _attn_step(q_ref, k_ref, vt_ref, lam_ref, sw_ref, o_ref, m_ref, l_ref, acc_ref, ki,
                   lambda_init=lambda_init, nk=nk, qp=qp)

    @pl.when((qi == nq) & (ki == nk - 1))
    def _():
        n_ctx = octx_ref.shape[0]
        o_ref[0:n_ctx, :] = octx_ref[...]
        o_ref[n_ctx:, :] = jnp.zeros((o_ref.shape[0] - n_ctx, o_ref.shape[1]), o_ref.dtype)


def _attn_step(q_ref, k_ref, vt_ref, lam_ref, sw_ref, o_ref, m_ref, l_ref, acc_ref, ki, *, lambda_init, nk, qp):
    @pl.when(ki == 0)
    def _():
        m_ref[...] = jnp.full(m_ref.shape, NEG_BIG, F32)
        l_ref[...] = jnp.zeros_like(l_ref)
        acc_ref[...] = jnp.zeros_like(acc_ref)

    tq = q_ref.shape[0]
    chains = [(c, j) for j in range(tq // qp) for c in range(2)]

    def scores(c, j):
        qc = q_ref[j * qp:(j + 1) * qp, c * DIFF_HEAD_DIM:(c + 1) * DIFF_HEAD_DIM]
        kc = k_ref[:, c * DIFF_HEAD_DIM:(c + 1) * DIFF_HEAD_DIM]
        return lax.dot_general(kc, qc, (((1,), (1,)), ((), ())), preferred_element_type=F32)

    s_next = scores(*chains[0])
    for n, (c, j) in enumerate(chains):
        s = s_next
        if n + 1 < len(chains):
            s_next = scores(*chains[n + 1])
        cols = slice(j * qp, (j + 1) * qp)
        m_prev = m_ref[c, :, cols]
        m_new = jnp.maximum(m_prev, jnp.max(s, axis=0, keepdims=True))
        alpha = jnp.exp2(m_prev - m_new)
        p = jnp.exp2(s - m_new)
        l_ref[c, :, cols] = alpha * l_ref[c, :, cols] + jnp.sum(p, axis=0, keepdims=True)
        acc_ref[c, :, cols] = alpha * acc_ref[c, :, cols] + jnp.dot(
            vt_ref[...], p.astype(BF16), preferred_element_type=F32)
        m_ref[c, :, cols] = m_new

    @pl.when(ki == nk - 1)
    def _():
        lam = _diff_lambda(lam_ref, lambda_init)
        ot = acc_ref[0] / l_ref[0] - lam * (acc_ref[1] / l_ref[1])
        o_ref[...] = _attn_finish(ot, sw_ref, lambda_init)


def _attn_ctx_kernel(q_ref, k_ref, vt_ref, lam_ref, sw_ref, o_ref, *, lambda_init):
    lam = _diff_lambda(lam_ref, lambda_init)
    outs = []
    for c in range(2):
        s = _attn_scores_t(q_ref, k_ref, c)
        p = jnp.exp2(s - jnp.max(s, axis=0, keepdims=True))
        l = jnp.sum(p, axis=0, keepdims=True)
        outs.append(jnp.dot(vt_ref[...], p.astype(BF16), preferred_element_type=F32) / l)
    o_ref[...] = _attn_finish(outs[0] - lam * outs[1], sw_ref, lambda_init)


def _attn_call(qk, vt, lam_vecs, subln_w, *, lambda_init, n_lat, n_ctx):
    d, t = vt.shape
    heads = d // HEAD_W
    tq = _pick(n_lat, (4096, 2048, 1024, 512, 256, 128))
    tk = _pick(t, (3328, 1280, 640, 256, 128))
    nq = n_lat // tq
    nk = t // tk
    cblk = n_lat // n_ctx
    assert n_ctx <= tq
    sw = subln_w.reshape(1, HEAD_W)
    small = lambda a: pl.BlockSpec(a.shape, lambda *_: (0,) * a.ndim)
    o_ctx = pl.pallas_call(
        functools.partial(_attn_ctx_kernel, lambda_init=lambda_init),
        grid=(heads,),
        in_specs=[
            pl.BlockSpec((n_ctx, HEAD_W), lambda h: (cblk, h)),
            pl.BlockSpec((n_ctx, HEAD_W), lambda h: (cblk, heads + h)),
            pl.BlockSpec((HEAD_W, n_ctx), lambda h: (h, cblk)),
            small(lam_vecs), small(sw),
        ],
        out_specs=pl.BlockSpec((n_ctx, HEAD_W), lambda h: (0, h)),
        out_shape=jax.ShapeDtypeStruct((n_ctx, d), BF16),
        compiler_params=_params(("parallel",)),
        name="diff_attn_ctx",
    )(qk, qk, vt, lam_vecs, sw)
    last_q = nq - 1
    kv_blk = lambda i, k: jnp.where(i < nq, k, nk - 1)
    return pl.pallas_call(
        functools.partial(_attn_kernel, lambda_init=lambda_init, nq=nq, nk=nk, qp=min(tq, 256)),
        grid=(heads, nq + 1, nk),
        in_specs=[
            pl.BlockSpec((tq, HEAD_W), lambda h, i, k: (jnp.minimum(i, last_q), h)),
            pl.BlockSpec((tk, HEAD_W), lambda h, i, k: (kv_blk(i, k), heads + h)),
            pl.BlockSpec((HEAD_W, tk), lambda h, i, k: (h, kv_blk(i, k))),
            pl.BlockSpec((n_ctx, HEAD_W), lambda h, i, k: (0, h)),
            small(lam_vecs), small(sw),
        ],
        out_specs=pl.BlockSpec((tq, HEAD_W), lambda h, i, k: (i, h)),
        out_shape=jax.ShapeDtypeStruct(((nq + 1) * tq, d), BF16),
        scratch_shapes=[pltpu.VMEM((2, 1, tq), F32), pltpu.VMEM((2, 1, tq), F32),
                        pltpu.VMEM((2, HEAD_W, tq), F32)],
        compiler_params=_params(("parallel", "parallel", "arbitrary")),
        name="diff_attn",
    )(qk, qk, vt, o_ctx, lam_vecs, sw)


def _moe_kernel(elo_ref, ehi_ref, src_ref, srcn_ref, dst_ref, x_hbm, w13l_ref, w13h_ref, w2l_ref, w2h_ref,
                y_hbm, xbuf, ybuf, gsem, ssem, *, ts, d, n_tiles):
    del elo_ref, ehi_ref
    i = pl.program_id(0)
    slot = i % 2
    nslot = 1 - slot

    def start_gather(idx_ref, sl):
        for r in range(ts):
            pltpu.make_async_copy(x_hbm.at[pl.ds(idx_ref[0, r], 1)], xbuf.at[sl, pl.ds(r, 1)],
                                  gsem.at[sl]).start(priority=r % 2)

    def wait_gather(sl):
        pltpu.make_async_copy(x_hbm.at[pl.ds(0, ts)], xbuf.at[sl], gsem.at[sl]).wait()

    def wait_scatter(sl):
        pltpu.make_async_copy(ybuf.at[sl], y_hbm.at[pl.ds(0, ts)], ssem.at[sl]).wait()

    @pl.when(i == 0)
    def _():
        start_gather(src_ref, 0)

    wait_gather(slot)

    @pl.when(i >= 2)
    def _():
        wait_scatter(slot)

    start_gather(srcn_ref, nslot)

    xin = xbuf[slot]
    hv = xin[:, :d].astype(BF16)
    g_lower = xin[:, d + 1:d + 2]
    g_higher = xin[:, d + 2:d + 3]
    f = w2l_ref.shape[0]

    def hidden(w13_ref, gate):
        au = jnp.dot(hv, w13_ref[...], preferred_element_type=F32)
        return ((_silu(au[:, :f]) * au[:, f:]) * gate).astype(BF16)

    ybuf[slot] = (jnp.dot(hidden(w13l_ref, g_lower), w2l_ref[...], preferred_element_type=F32)
                  + jnp.dot(hidden(w13h_ref, g_higher), w2h_ref[...], preferred_element_type=F32))

    for r in range(ts):
        pltpu.make_async_copy(ybuf.at[slot, pl.ds(r, 1)], y_hbm.at[pl.ds(dst_ref[0, r], 1)],
                              ssem.at[slot]).start(priority=r % 2)

    @pl.when(i == n_tiles - 1)
    def _():
        wait_scatter(slot)
        wait_scatter(nslot)
        wait_gather(nslot)


def _w13_kernel(w1_ref, w3_ref, o_ref):
    f = w1_ref.shape[-1]
    o_ref[:, :f] = w1_ref[...].astype(BF16)
    o_ref[:, f:] = w3_ref[...].astype(BF16)


def _w13_call(w1, w3):
    depth, n_e, d, f = w1.shape
    tr = _pick(d, (2048, 1024, 512, 256))
    spec_in = pl.BlockSpec((None, None, tr, f), lambda l, e, r: (l, e, r, 0))
    return pl.pallas_call(
        _w13_kernel,
        grid=(depth, n_e, d // tr),
        in_specs=[spec_in, spec_in],
        out_specs=pl.BlockSpec((None, None, tr, 2 * f), lambda l, e, r: (l, e, r, 0)),
        out_shape=jax.ShapeDtypeStruct((depth, n_e, d, 2 * f), BF16),
        compiler_params=_params(("parallel", "parallel", "parallel")),
        name="w13_pack",
    )(w1, w3)


def _moe_plan(info, *, ts, n_pad_rows):
    t = info.shape[0]
    n_buckets = N_EXPERT_GROUPS * PAIRS_PER_GROUP
    bucket = info[:, 0].astype(jnp.int32)
    onehot = (bucket[:, None] == jnp.arange(n_buckets, dtype=jnp.int32)[None, :]).astype(jnp.int32)
    csum = jnp.cumsum(onehot, axis=0)
    rank = jnp.sum(csum * onehot, axis=1) - 1
    counts = csum[-1]
    padded = ((counts + ts - 1) // ts) * ts
    ends = jnp.cumsum(padded)
    starts = ends - padded
    pos = jnp.sum(starts[None, :] * onehot, axis=1) + rank
    tok = jnp.arange(t, dtype=jnp.int32)
    slot_tok = jnp.full((n_pad_rows,), -1, jnp.int32).at[pos].set(tok)
    valid = slot_tok >= 0
    src = jnp.maximum(slot_tok, 0)
    pad_rank = jnp.cumsum(1 - valid.astype(jnp.int32)) - 1
    dst = jnp.where(valid, slot_tok, t + pad_rank).astype(jnp.int32)
    n_tiles = n_pad_rows // ts
    tile_bucket = jnp.sum((jnp.arange(n_tiles, dtype=jnp.int32)[:, None] * ts >= ends[None, :]).astype(jnp.int32),
                          axis=1)
    tile_bucket = jnp.minimum(tile_bucket, n_buckets - 1)
    pair = tile_bucket % PAIRS_PER_GROUP
    pair_lo = jnp.array([0, 0, 0, 1, 1, 2], jnp.int32)[pair]
    pair_hi = jnp.array([1, 2, 3, 2, 3, 3], jnp.int32)[pair]
    base = (tile_bucket // PAIRS_PER_GROUP) * EXPERTS_PER_GROUP
    return (src.reshape(n_tiles, 1, ts), dst.reshape(n_tiles, 1, ts),
            (base + pair_lo).astype(jnp.int32), (base + pair_hi).astype(jnp.int32))


def _moe_call(hx, info, w13, w2, layer):
    t, dx = hx.shape
    d = dx - LANES
    f = w2.shape[2]
    ts = MOE_TILE
    n_pad_rows = t + N_EXPERT_GROUPS * PAIRS_PER_GROUP * ts
    n_tiles = n_pad_rows // ts
    src, dst, e_lo, e_hi = _moe_plan(info, ts=ts, n_pad_rows=n_pad_rows)
    smem_tile = lambda fn: pl.BlockSpec((None, 1, ts), fn, memory_space=pltpu.SMEM)
    w_in = lambda pick: pl.BlockSpec((None, None, d, 2 * f), lambda i, lo, hi: (layer, pick(lo, hi)[i], 0, 0))
    w_out = lambda pick: pl.BlockSpec((None, None, f, d), lambda i, lo, hi: (layer, pick(lo, hi)[i], 0, 0))
    lower = lambda lo, hi: lo
    higher = lambda lo, hi: hi
    grid_spec = pltpu.PrefetchScalarGridSpec(
        num_scalar_prefetch=2,
        grid=(n_tiles,),
        in_specs=[
            smem_tile(lambda i, lo, hi: (i, 0, 0)),
            smem_tile(lambda i, lo, hi: (jnp.minimum(i + 1, n_tiles - 1), 0, 0)),
            smem_tile(lambda i, lo, hi: (i, 0, 0)),
            pl.BlockSpec(memory_space=pl.ANY),
            w_in(lower), w_in(higher), w_out(lower), w_out(higher),
        ],
        out_specs=pl.BlockSpec(memory_space=pl.ANY),
        scratch_shapes=[pltpu.VMEM((2, ts, dx), F32), pltpu.VMEM((2, ts, d), F32),
                        pltpu.SemaphoreType.DMA((2,)), pltpu.SemaphoreType.DMA((2,))],
    )
    return pl.pallas_call(
        functools.partial(_moe_kernel, ts=ts, d=d, n_tiles=n_tiles),
        grid_spec=grid_spec,
        out_shape=jax.ShapeDtypeStruct((n_pad_rows, d), F32),
        compiler_params=_params(("arbitrary",)),
        name="moe_routed",
    )(e_lo, e_hi, src, src, dst, hx, w13, w13, w2, w2)


def _final_kernel(x_ref, y_ref, gmod_ref, fw_ref, o_ref):
    xv = x_ref[...] + gmod_ref[5:6, :] * y_ref[...]
    r = lax.rsqrt(jnp.mean(xv * xv, axis=-1, keepdims=True) + RMS_EPS)
    o_ref[...] = (xv * r) * fw_ref[...]


def _final_call(x, y, gmod, fw, *, n_lat):
    _, d = x.shape
    tm = 256
    row = pl.BlockSpec((tm, d), lambda i: (i, 0))
    full = lambda a: pl.BlockSpec(a.shape, lambda i: (0,) * a.ndim)
    return pl.pallas_call(
        _final_kernel,
        grid=(n_lat // tm,),
        in_specs=[row, row, full(gmod), full(fw)],
        out_specs=row,
        out_shape=jax.ShapeDtypeStruct((n_lat, d), F32),
        compiler_params=_params(("parallel",)),
        name="final_norm",
    )(x, y, gmod, fw)


def _rope_tables(n_lat, n_ctx):
    axis_dim = DIFF_HEAD_DIM // 2
    quarter = axis_dim // 2
    rows = n_lat // GRID_W
    row = jnp.broadcast_to(jnp.arange(rows, dtype=F32)[:, None], (rows, GRID_W)).reshape(-1)
    col = jnp.broadcast_to(jnp.arange(GRID_W, dtype=F32)[None, :], (rows, GRID_W)).reshape(-1)
    inv_freq = ROPE_THETA ** (-jnp.arange(0, axis_dim, 2, dtype=F32) / axis_dim)
    ang_r = row[:, None] * inv_freq
    ang_c = col[:, None] * inv_freq
    zeros = jnp.zeros((n_lat, quarter), F32)
    c = jnp.concatenate([jnp.cos(ang_r), jnp.cos(ang_r), jnp.cos(ang_c), jnp.cos(ang_c)], axis=1)
    sa = jnp.concatenate([-jnp.sin(ang_r), zeros, -jnp.sin(ang_c), zeros], axis=1)
    sb = jnp.concatenate([zeros, jnp.sin(ang_r), zeros, jnp.sin(ang_c)], axis=1)
    pad = lambda a, v: jnp.concatenate([a, jnp.full((n_ctx, DIFF_HEAD_DIM), v, F32)], axis=0)
    return pad(c, 1.0), pad(sa, 0.0), pad(sb, 0.0)


def _pack_mod(m, d):
    z = jnp.zeros((MOD_CTX_ROW - N_MOD, d), F32)
    return jnp.concatenate([m[0].reshape(N_MOD, d), z, m[1].reshape(N_MOD, d), z], axis=0)


def kernel(x, c, ctx, c_ctx, ada_w, ada_b, norm_w, pool_w, pool_b, pool_scale, attn_w_qkv, attn_w_o,
           attn_lambda, attn_subln_w, router_group_w, router_expert_w, expert_w1, expert_w3, expert_w2,
           final_norm_w):
    _, n_lat, d = x.shape
    n_ctx = ctx.shape[1]
    depth = ada_w.shape[0]
    assert x.shape[0] == 1 and n_lat % n_ctx == 0 and n_ctx % 256 == 0 and d % (4 * LANES) == 0

    mods = _adaln(c, c_ctx, ada_w, ada_b)
    mod = [_pack_mod(mods[i], d) for i in range(depth)]

    wr = jnp.concatenate([router_group_w, router_expert_w], axis=-1)
    wr = jnp.pad(wr, ((0, 0), (0, 0), (0, LANES - wr.shape[-1])))
    wr_hi = wr.astype(BF16)
    wr_lo = (wr - wr_hi.astype(F32)).astype(BF16)
    wr2 = jnp.concatenate([wr_hi, wr_lo], axis=-1)

    pool_w16 = pool_w.astype(BF16)
    wqkv16 = attn_w_qkv.astype(BF16)
    wo16 = attn_w_o.astype(BF16)
    w13_16 = _w13_call(expert_w1, expert_w3)
    w2_16 = expert_w2.astype(BF16)
    rope_c, rope_sa, rope_sb = _rope_tables(n_lat, n_ctx)

    xs, y = x[0], None
    for i in range(depth):
        jm = i // 2
        gprev = mod[i - 1] if i > 0 else None
        if i % 2 == 0:
            xs, h2, gates = _pool_call(xs, y, gprev, mod[i], norm_w[i], pool_w16, pool_b[jm],
                                       pool_scale[jm].reshape(1, d), wr2, n_lat=n_lat, layer=i, mixer=jm,
                                       ctx=ctx[0] if i == 0 else None)
        else:
            lambda_init = 0.8 - 0.6 * math.exp(-0.3 * i)
            xs, h, _ = _norm_call(xs, y, gprev, mod[i], norm_w[i, 0:1], None, k_shift=0, n_lat=n_lat)
            qk, vt = _qkv_call(h, wqkv16, jm, rope_c, rope_sa, rope_sb)
            o = _attn_call(qk, vt, attn_lambda[jm], attn_subln_w[jm], lambda_init=lambda_init,
                           n_lat=n_lat, n_ctx=n_ctx)
            xs = _oproj_call(o, wo16, jm, xs, mod[i], n_lat=n_lat)
            _, h2, gates = _norm_call(xs, None, None, mod[i], norm_w[i, 1:2], wr2, k_shift=3, n_lat=n_lat,
                                      layer=i)
        y = _moe_call(h2, gates, w13_16, w2_16, i)
    out = _final_call(xs, y, mod[depth - 1], final_norm_w.reshape(1, d), n_lat=n_lat)
    return out[None]
```

```python
import functools
import math

import jax
import jax.numpy as jnp
from jax import lax
from jax.experimental import pallas as pl
from jax.experimental.pallas import tpu as pltpu

F32 = jnp.float32
BF16 = jnp.bfloat16

GRID_W = 64
RMS_EPS = 1e-6
ROPE_THETA = 10000.0
POOL_WINDOWS = (2, 4, 8, 16)
N_MOD = 6
N_EXPERT_GROUPS = 4
EXPERTS_PER_GROUP = 4
N_EXPERTS = N_EXPERT_GROUPS * EXPERTS_PER_GROUP
DIFF_HEAD_DIM = 128
HEAD_W = 2 * DIFF_HEAD_DIM
LANES = 128
SUBLANES = 8
MXU_DIM = 256
MOD_CTX_ROW = 8
ROUTER_COL0 = N_EXPERT_GROUPS
PAIRS_PER_GROUP = 6
MOE_TILE = 128
ADALN_COLS = 2048
VMEM_LIMIT = 56 * 1024 * 1024
NEG_BIG = -1e30
LOG2E = 1.4426950408889634


def _pick(n, candidates):
    for c in candidates:
        if c <= n and n % c == 0:
            return c
    return n


def _params(semantics):
    return pltpu.CompilerParams(dimension_semantics=semantics, vmem_limit_bytes=VMEM_LIMIT)


def _layer_spec(stacked, layer):
    return pl.BlockSpec((None,) + stacked.shape[1:], lambda *_: (layer,) + (0,) * (stacked.ndim - 1))


def _silu(v):
    return v * jax.nn.sigmoid(v)


def _adaln_kernel(c_ref, cc_ref, w_ref, b_ref, o_ref, acc_ref, s_ref, *, tk, tn, nk):
    k = pl.program_id(2)

    @pl.when(k == 0)
    def _():
        acc_ref[...] = jnp.zeros_like(acc_ref)

    s_ref[0] = _silu(c_ref[...])
    s_ref[1] = _silu(cc_ref[...])

    cw = min(tn, ADALN_COLS)
    reps = cw // LANES
    for c0 in range(0, tn, cw):
        def body(j, carry, c0=c0):
            a0, a1 = carry
            r = pl.multiple_of(j * SUBLANES, SUBLANES)
            w = w_ref[pl.ds(r, SUBLANES), c0:c0 + cw]
            a0 = a0 + w * jnp.concatenate([s_ref[0, pl.ds(r, SUBLANES), :]] * reps, axis=1)
            a1 = a1 + w * jnp.concatenate([s_ref[1, pl.ds(r, SUBLANES), :]] * reps, axis=1)
            return a0, a1

        z = jnp.zeros((SUBLANES, cw), F32)
        a0, a1 = lax.fori_loop(0, tk // SUBLANES, body, (z, z), unroll=4)
        acc_ref[0, :, c0:c0 + cw] += a0
        acc_ref[1, :, c0:c0 + cw] += a1

    @pl.when(k == nk - 1)
    def _():
        o_ref[0:1, :] = jnp.sum(acc_ref[0], axis=0, keepdims=True) + b_ref[...]
        o_ref[1:2, :] = jnp.sum(acc_ref[1], axis=0, keepdims=True) + b_ref[...]


def _adaln(c, c_ctx, ada_w, ada_b):
    depth, d, n = ada_w.shape
    tk = _pick(d, (1024, 512, 256, 128))
    tn = _pick(n, (4096, 2048, 1024, 512, 256, 128))
    nk = d // tk
    c_b = jnp.broadcast_to(c.reshape(d, 1), (d, LANES))
    cc_b = jnp.broadcast_to(c_ctx.reshape(d, 1), (d, LANES))
    return pl.pallas_call(
        functools.partial(_adaln_kernel, tk=tk, tn=tn, nk=nk),
        grid=(depth, n // tn, nk),
        in_specs=[
            pl.BlockSpec((tk, LANES), lambda l, j, k: (k, 0)),
            pl.BlockSpec((tk, LANES), lambda l, j, k: (k, 0)),
            pl.BlockSpec((None, tk, tn), lambda l, j, k: (l, k, j)),
            pl.BlockSpec((None, 1, tn), lambda l, j, k: (l, 0, j)),
        ],
        out_specs=pl.BlockSpec((None, 2, tn), lambda l, j, k: (l, 0, j)),
        out_shape=jax.ShapeDtypeStruct((depth, 2, n), F32),
        scratch_shapes=[pltpu.VMEM((2, SUBLANES, tn), F32), pltpu.VMEM((2, tk, LANES), F32)],
        compiler_params=_params(("parallel", "parallel", "arbitrary")),
        name="adaln",
    )(c_b, cc_b, ada_w, ada_b.reshape(depth, 1, n))


def _mod_row(mod_ref, k, is_ctx):
    return jnp.where(is_ctx, mod_ref[MOD_CTX_ROW + k:MOD_CTX_ROW + k + 1, :], mod_ref[k:k + 1, :])


def _norm_mod(xv, w_row, shift, scale):
    r = lax.rsqrt(jnp.mean(xv * xv, axis=-1, keepdims=True) + RMS_EPS)
    return (xv * r) * (w_row * (1.0 + scale)) + shift


def _router_info(h, h_hi, wr_ref):
    h_lo = (h - h_hi.astype(F32)).astype(BF16)
    r1 = jnp.dot(h_hi, wr_ref[...], preferred_element_type=F32)
    r2 = jnp.dot(h_lo, wr_ref[:, :LANES], preferred_element_type=F32)
    logits = r1[:, :LANES] + r1[:, LANES:] + r2
    col = lax.broadcasted_iota(jnp.int32, logits.shape, 1).astype(F32)
    far = float(4 * LANES)
    is_g = col < float(N_EXPERT_GROUPS)
    mg = jnp.max(jnp.where(is_g, logits, NEG_BIG), axis=1, keepdims=True)
    denom = jnp.sum(jnp.where(is_g, jnp.exp(logits - mg), 0.0), axis=1, keepdims=True)
    pg_top = 1.0 / denom
    g_sel = jnp.min(jnp.where(is_g & (logits == mg), col, far), axis=1, keepdims=True)
    e_lo = float(ROUTER_COL0) + float(EXPERTS_PER_GROUP) * g_sel
    in_grp = (col >= e_lo) & (col < e_lo + float(EXPERTS_PER_GROUP))
    v1 = jnp.max(jnp.where(in_grp, logits, NEG_BIG), axis=1, keepdims=True)
    i1 = jnp.min(jnp.where(in_grp & (logits == v1), col, far), axis=1, keepdims=True)
    rest = in_grp & (col != i1)
    v2 = jnp.max(jnp.where(rest, logits, NEG_BIG), axis=1, keepdims=True)
    i2 = jnp.min(jnp.where(rest & (logits == v2), col, far), axis=1, keepdims=True)
    e2 = jnp.exp(v2 - v1)
    w1 = pg_top / (1.0 + e2)
    w2 = pg_top * (e2 / (1.0 + e2))
    first_lower = i1 < i2
    ja = jnp.minimum(i1, i2) - e_lo
    jb = jnp.maximum(i1, i2) - e_lo
    bucket = g_sel * float(PAIRS_PER_GROUP) + ja * (7.0 - ja) * 0.5 + (jb - ja - 1.0)
    g_lower = jnp.where(first_lower, w1, w2)
    g_higher = jnp.where(first_lower, w2, w1)
    return jnp.where(col == 0.0, bucket, jnp.where(col == 1.0, g_lower, jnp.where(col == 2.0, g_higher, 0.0)))


def _emit_routed_rows(h, wr_ref, hx_ref, info_ref):
    d = h.shape[1]
    info = _router_info(h, h.astype(BF16), wr_ref)
    hx_ref[:, :d] = h
    hx_ref[:, d:] = info
    info_ref[...] = info


def _norm_kernel(*refs, has_y, router, k_shift, lat_tiles):
    it = iter(refs)
    x_ref = next(it)
    y_ref = next(it) if has_y else None
    gmod_ref = next(it) if has_y else None
    mod_ref = next(it)
    nw_ref = next(it)
    wr_ref = next(it) if router else None
    xo_ref = next(it) if has_y else None
    h_ref = next(it)
    g_ref = next(it) if router else None

    is_ctx = pl.program_id(0) >= lat_tiles
    xv = x_ref[...]
    if has_y:
        xv = xv + _mod_row(gmod_ref, 5, is_ctx) * y_ref[...]
        xo_ref[...] = xv
    h = _norm_mod(xv, nw_ref[...], _mod_row(mod_ref, k_shift, is_ctx), _mod_row(mod_ref, k_shift + 1, is_ctx))
    if router:
        _emit_routed_rows(h, wr_ref, h_ref, g_ref)
    else:
        h_ref[...] = h.astype(BF16)


def _norm_call(x, y, gmod, mod, nw, wr, *, k_shift, n_lat, layer=0):
    t, d = x.shape
    tm = 256
    has_y = y is not None
    router = wr is not None
    row = pl.BlockSpec((tm, d), lambda i: (i, 0))
    full = lambda a: pl.BlockSpec(a.shape, lambda i: (0,) * a.ndim)
    ins, specs = [x], [row]
    if has_y:
        ins += [y, gmod]
        specs += [row, full(gmod)]
    ins += [mod, nw]
    specs += [full(mod), full(nw)]
    if router:
        ins.append(wr)
        specs.append(_layer_spec(wr, layer))
    outs, ospecs = [], []
    if has_y:
        outs.append(jax.ShapeDtypeStruct((t, d), F32))
        ospecs.append(row)
    if router:
        outs += [jax.ShapeDtypeStruct((t, d + LANES), F32), jax.ShapeDtypeStruct((t, LANES), F32)]
        ospecs += [pl.BlockSpec((tm, d + LANES), lambda i: (i, 0)), pl.BlockSpec((tm, LANES), lambda i: (i, 0))]
    else:
        outs.append(jax.ShapeDtypeStruct((t, d), BF16))
        ospecs.append(row)
    res = pl.pallas_call(
        functools.partial(_norm_kernel, has_y=has_y, router=router, k_shift=k_shift, lat_tiles=n_lat // tm),
        grid=(t // tm,),
        in_specs=specs,
        out_specs=ospecs,
        out_shape=outs,
        compiler_params=_params(("parallel",)),
        name="norm_mod",
    )(*ins)
    res = list(res)
    xo = res.pop(0) if has_y else x
    h = res.pop(0)
    g = res.pop(0) if router else None
    return xo, h, g


HALO = 8


def _pool_kernel(*refs, has_y, split_in, tp, cg, lat_tiles, n_tiles):
    it = iter(refs)
    x_ref, xp_ref, xn_ref = next(it), next(it), next(it)
    if split_in:
        c_ref, cp_ref, cn_ref = next(it), next(it), next(it)
    if has_y:
        y_ref, yp_ref, yn_ref, gmod_ref = next(it), next(it), next(it), next(it)
    mod_ref, nw_ref, pw_ref, pb_ref, ps_ref, wr_ref = (next(it) for _ in range(6))
    x1_ref, h2_ref, g_ref = next(it), next(it), next(it)
    hext_ref, xin_ref = next(it), next(it)

    j = pl.program_id(0)
    is_ctx = j >= lat_tiles
    first = (j == 0) | (j == lat_tiles)
    last = (j == lat_tiles - 1) | (j == n_tiles - 1)

    nw = nw_ref[0:1, :]
    ws = nw * (1.0 + _mod_row(mod_ref, 1, is_ctx))
    shift = _mod_row(mod_ref, 0, is_ctx)

    def x_in(xr, yr, cr):
        xv = xr[...]
        if split_in:
            xv = jnp.where(is_ctx, cr[...], xv)
        if has_y:
            xv = xv + _mod_row(gmod_ref, 5, is_ctx) * yr[...]
        return xv

    def h_of(xv):
        r = lax.rsqrt(jnp.mean(xv * xv, axis=-1, keepdims=True) + RMS_EPS)
        return (xv * r) * ws + shift

    xm = x_in(x_ref, y_ref if has_y else None, c_ref if split_in else None)
    xin_ref[...] = xm
    hext_ref[HALO:HALO + tp, :] = h_of(xm)
    hp = h_of(x_in(xp_ref, yp_ref if has_y else None, cp_ref if split_in else None))
    hext_ref[0:HALO, :] = jnp.where(first, 0.0, hp)
    hn = h_of(x_in(xn_ref, yn_ref if has_y else None, cn_ref if split_in else None))
    hext_ref[HALO + tp:HALO + tp + HALO, :] = jnp.where(last, 0.0, hn)

    tpos = lax.broadcasted_iota(jnp.int32, (tp, 1), 0).astype(F32)
    lo_lim = jnp.where(first, 0.0, -float(2 * HALO))
    hi_lim = jnp.where(last, float(tp), float(tp + 2 * HALO))
    gate = _mod_row(mod_ref, 2, is_ctx)
    live = tp + 2 * HALO
    kdim = -(-live // MXU_DIM) * MXU_DIM
    win_t = lax.broadcasted_iota(jnp.int32, (tp, kdim), 0)
    win_s = lax.broadcasted_iota(jnp.int32, (tp, kdim), 1)
    kpad = jnp.zeros((kdim - live, cg), BF16)
    for g, w in enumerate(POOL_WINDOWS):
        c0 = g * cg
        half = w // 2
        band = jnp.where((win_s >= win_t + (HALO - half)) & (win_s < win_t + (HALO + half)), 1.0, 0.0).astype(BF16)
        hx = hext_ref[:, c0:c0 + cg]
        hx_hi = hx.astype(BF16)
        hx_lo = (hx - hx_hi.astype(F32)).astype(BF16)
        wsum = (jnp.dot(band, jnp.concatenate([hx_hi, kpad], axis=0), preferred_element_type=F32)
                + jnp.dot(band, jnp.concatenate([hx_lo, kpad], axis=0), preferred_element_type=F32))
        cnt = jnp.minimum(tpos + float(half), hi_lim) - jnp.maximum(tpos - float(half), lo_lim)
        pooled = wsum / cnt - hext_ref[HALO:HALO + tp, c0:c0 + cg]
        yg = jnp.dot(pooled.astype(BF16), pw_ref[g], preferred_element_type=F32) + pb_ref[g:g + 1, :]
        yg = yg * ps_ref[:, c0:c0 + cg]
        x1_ref[:, c0:c0 + cg] = xin_ref[:, c0:c0 + cg] + gate[:, c0:c0 + cg] * yg

    x1 = x1_ref[...]
    h2 = _norm_mod(x1, nw_ref[1:2, :], _mod_row(mod_ref, 3, is_ctx), _mod_row(mod_ref, 4, is_ctx))
    _emit_routed_rows(h2, wr_ref, h2_ref, g_ref)


def _pool_call(x, y, gmod, mod, nw2, pw, pb, ps, wr, *, n_lat, layer, mixer, ctx=None):
    d = x.shape[1]
    split_in = ctx is not None
    t = x.shape[0] + (ctx.shape[0] if split_in else 0)
    tp = 128
    cg = d // len(POOL_WINDOWS)
    has_y = y is not None
    n_tiles = t // tp
    lat_tiles = n_lat // tp
    bpt = tp // HALO

    def halo_specs(n_rows, tile0):
        n_t, last_blk = n_rows // tp, n_rows // HALO - 1
        clip = lambda v, hi: jnp.minimum(jnp.maximum(v, 0), hi)
        return [pl.BlockSpec((tp, d), lambda i: (clip(i - tile0, n_t - 1), 0)),
                pl.BlockSpec((HALO, d), lambda i: (clip((i - tile0) * bpt - 1, last_blk), 0)),
                pl.BlockSpec((HALO, d), lambda i: (clip((i - tile0 + 1) * bpt, last_blk), 0))]

    row = pl.BlockSpec((tp, d), lambda i: (i, 0))
    full = lambda a: pl.BlockSpec(a.shape, lambda i: (0,) * a.ndim)
    ins, specs = [x, x, x], halo_specs(x.shape[0], 0)
    if split_in:
        ins += [ctx, ctx, ctx]
        specs += halo_specs(ctx.shape[0], lat_tiles)
    _, prev, nxt = halo_specs(t, 0)
    if has_y:
        ins += [y, y, y, gmod]
        specs += [row, prev, nxt, full(gmod)]
    ins += [mod, nw2, pw, pb, ps, wr]
    specs += [full(mod), full(nw2), _layer_spec(pw, mixer), full(pb), full(ps), _layer_spec(wr, layer)]
    return pl.pallas_call(
        functools.partial(_pool_kernel, has_y=has_y, split_in=split_in, tp=tp, cg=cg, lat_tiles=lat_tiles,
                          n_tiles=n_tiles),
        grid=(n_tiles,),
        in_specs=specs,
        out_specs=[row, pl.BlockSpec((tp, d + LANES), lambda i: (i, 0)), pl.BlockSpec((tp, LANES), lambda i: (i, 0))],
        out_shape=[jax.ShapeDtypeStruct((t, d), F32), jax.ShapeDtypeStruct((t, d + LANES), F32),
                   jax.ShapeDtypeStruct((t, LANES), F32)],
        scratch_shapes=[pltpu.VMEM((tp + 2 * HALO, d), F32), pltpu.VMEM((tp, d), F32)],
        compiler_params=_params(("parallel",)),
        name="pool_layer",
    )(*ins)


def _qk_kernel(a_ref, w_ref, c_ref, sa_ref, sb_ref, o_ref, *, d, tn, qscale):
    acc = jnp.dot(a_ref[...], w_ref[...], preferred_element_type=F32)
    f = jnp.where(pl.program_id(1) < d // tn, qscale, 1.0)
    cc = c_ref[...] * f
    sa = sa_ref[...] * f
    sb = sb_ref[...] * f
    for b in range(tn // LANES):
        blk = acc[:, b * LANES:(b + 1) * LANES]
        rot = blk * cc + pltpu.roll(blk, 96, 1) * sa + pltpu.roll(blk, 32, 1) * sb
        o_ref[:, b * LANES:(b + 1) * LANES] = rot.astype(BF16)


def _vt_kernel(a_ref, w_ref, o_ref, acc_ref):
    acc_ref[...] = jnp.dot(a_ref[...], w_ref[...], preferred_element_type=F32)
    o_ref[...] = acc_ref[...].T.astype(BF16)


def _qkv_call(h, w, mixer, rope_c, rope_sa, rope_sb):
    t, d = h.shape
    tm = _pick(t, (640, 512, 256, 128))
    tn = _pick(d, (1024, 512, 256))
    qscale = (DIFF_HEAD_DIM ** -0.5) * LOG2E
    tab = pl.BlockSpec((tm, LANES), lambda i, j: (i, 0))
    a_spec = pl.BlockSpec((tm, d), lambda i, j: (i, 0))
    qk = pl.pallas_call(
        functools.partial(_qk_kernel, d=d, tn=tn, qscale=qscale),
        grid=(t // tm, (2 * d) // tn),
        in_specs=[a_spec, pl.BlockSpec((None, d, tn), lambda i, j: (mixer, 0, j)), tab, tab, tab],
        out_specs=pl.BlockSpec((tm, tn), lambda i, j: (i, j)),
        out_shape=jax.ShapeDtypeStruct((t, 2 * d), BF16),
        compiler_params=_params(("parallel", "arbitrary")),
        name="qk_proj",
    )(h, w, rope_c, rope_sa, rope_sb)
    v_col0 = (2 * d) // tn
    vt = pl.pallas_call(
        _vt_kernel,
        grid=(t // tm, d // tn),
        in_specs=[a_spec, pl.BlockSpec((None, d, tn), lambda i, j: (mixer, 0, v_col0 + j))],
        out_specs=pl.BlockSpec((tn, tm), lambda i, j: (j, i)),
        out_shape=jax.ShapeDtypeStruct((d, t), BF16),
        scratch_shapes=[pltpu.VMEM((tm, tn), F32)],
        compiler_params=_params(("parallel", "arbitrary")),
        name="v_proj_t",
    )(h, w)
    return qk, vt


def _oproj_kernel(a_ref, w_ref, x_ref, mod_ref, o_ref, *, tm, n_lat):
    acc = jnp.dot(a_ref[...], w_ref[...], preferred_element_type=F32)
    rows = pl.program_id(0) * tm + lax.broadcasted_iota(jnp.int32, (tm, 1), 0)
    gate = jnp.where(rows >= n_lat, mod_ref[MOD_CTX_ROW + 2:MOD_CTX_ROW + 3, :], mod_ref[2:3, :])
    o_ref[...] = x_ref[...] + gate * acc


def _oproj_call(o, w, mixer, x, mod, *, n_lat):
    t, d = x.shape
    tm = _pick(t, (640, 512, 256, 128))
    tn = _pick(d, (1024, 512, 256))
    return pl.pallas_call(
        functools.partial(_oproj_kernel, tm=tm, n_lat=n_lat),
        grid=(t // tm, d // tn),
        in_specs=[pl.BlockSpec((tm, d), lambda i, j: (i, 0)),
                  pl.BlockSpec((None, d, tn), lambda i, j: (mixer, 0, j)),
                  pl.BlockSpec((tm, tn), lambda i, j: (i, j)), pl.BlockSpec((16, tn), lambda i, j: (0, j))],
        out_specs=pl.BlockSpec((tm, tn), lambda i, j: (i, j)),
        out_shape=jax.ShapeDtypeStruct((t, d), F32),
        compiler_params=_params(("parallel", "arbitrary")),
        name="attn_out_proj",
    )(o, w, x, mod)


def _diff_lambda(lam_ref, lambda_init):
    lv = lam_ref[...]
    a = jnp.sum(lv[0:1, :] * lv[1:2, :], axis=1, keepdims=True)
    b = jnp.sum(lv[2:3, :] * lv[3:4, :], axis=1, keepdims=True)
    return jnp.exp(a) - jnp.exp(b) + lambda_init


def _attn_scores_t(q_ref, k_ref, c):
    qc = q_ref[:, c * DIFF_HEAD_DIM:(c + 1) * DIFF_HEAD_DIM]
    kc = k_ref[:, c * DIFF_HEAD_DIM:(c + 1) * DIFF_HEAD_DIM]
    return lax.dot_general(kc, qc, (((1,), (1,)), ((), ())), preferred_element_type=F32)


def _attn_finish(ot, sw_ref, lambda_init):
    r = lax.rsqrt(jnp.mean(ot * ot, axis=0, keepdims=True) + RMS_EPS)
    return (((ot * r).T * sw_ref[...]) * (1.0 - lambda_init)).astype(BF16)


def _attn_kernel(q_ref, k_ref, vt_ref, octx_ref, lam_ref, sw_ref, o_ref, m_ref, l_ref, acc_ref,
                 *, lambda_init, nq, nk, qp):
    qi = pl.program_id(1)
    ki = pl.program_id(2)

    @pl.when(qi < nq)
    def _():
        _attn_step(q_ref, k_ref, vt_ref, lam_ref, sw_ref, o_ref, m_ref, l_ref, acc_ref, ki,
                   lambda_init=lambda_init, nk=nk, qp=qp)

    @pl.when((qi == nq) & (ki == nk - 1))
    def _():
        n_ctx = octx_ref.shape[0]
        o_ref[0:n_ctx, :] = octx_ref[...]
        o_ref[n_ctx:, :] = jnp.zeros((o_ref.shape[0] - n_ctx, o_ref.shape[1]), o_ref.dtype)


def _attn_step(q_ref, k_ref, vt_ref, lam_ref, sw_ref, o_ref, m_ref, l_ref, acc_ref, ki, *, lambda_init, nk, qp):
    @pl.when(ki == 0)
    def _():
        m_ref[...] = jnp.full(m_ref.shape, NEG_BIG, F32)
        l_ref[...] = jnp.zeros_like(l_ref)
        acc_ref[...] = jnp.zeros_like(acc_ref)

    tq = q_ref.shape[0]
    chains = [(c, j) for j in range(tq // qp) for c in range(2)]

    def scores(c, j):
        qc = q_ref[j * qp:(j + 1) * qp, c * DIFF_HEAD_DIM:(c + 1) * DIFF_HEAD_DIM]
        kc = k_ref[:, c * DIFF_HEAD_DIM:(c + 1) * DIFF_HEAD_DIM]
        return lax.dot_general(kc, qc, (((1,), (1,)), ((), ())), preferred_element_type=F32)

    s_next = scores(*chains[0])
    for n, (c, j) in enumerate(chains):
        s = s_next
        if n + 1 < len(chains):
            s_next = scores(*chains[n + 1])
        cols = slice(j * qp, (j + 1) * qp)
        m_prev = m_ref[c, :, cols]
        m_new = jnp.maximum(m_prev, jnp.max(s, axis=0, keepdims=True))
        alpha = jnp.exp2(m_prev - m_new)
        p = jnp.exp2(s - m_new)
        l_ref[c, :, cols] = alpha * l_ref[c, :, cols] + jnp.sum(p, axis=0, keepdims=True)
        acc_ref[c, :, cols] = alpha * acc_ref[c, :, cols] + jnp.dot(
            vt_ref[...], p.astype(BF16), preferred_element_type=F32)
        m_ref[c, :, cols] = m_new

    @pl.when(ki == nk - 1)
    def _():
        lam = _diff_lambda(lam_ref, lambda_init)
        ot = acc_ref[0] / l_ref[0] - lam * (acc_ref[1] / l_ref[1])
        o_ref[...] = _attn_finish(ot, sw_ref, lambda_init)


def _attn_ctx_kernel(q_ref, k_ref, vt_ref, lam_ref, sw_ref, o_ref, *, lambda_init):
    lam = _diff_lambda(lam_ref, lambda_init)
    outs = []
    for c in range(2):
        s = _attn_scores_t(q_ref, k_ref, c)
        p = jnp.exp2(s - jnp.max(s, axis=0, keepdims=True))
        l = jnp.sum(p, axis=0, keepdims=True)
        outs.append(jnp.dot(vt_ref[...], p.astype(BF16), preferred_element_type=F32) / l)
    o_ref[...] = _attn_finish(outs[0] - lam * outs[1], sw_ref, lambda_init)


def _attn_call(qk, vt, lam_vecs, subln_w, *, lambda_init, n_lat, n_ctx):
    d, t = vt.shape
    heads = d // HEAD_W
    tq = _pick(n_lat, (4096, 2048, 1024, 512, 256, 128))
    tk = _pick(t, (3328, 1280, 640, 256, 128))
    nq = n_lat // tq
    nk = t // tk
    cblk = n_lat // n_ctx
    assert n_ctx <= tq
    sw = subln_w.reshape(1, HEAD_W)
    small = lambda a: pl.BlockSpec(a.shape, lambda *_: (0,) * a.ndim)
    o_ctx = pl.pallas_call(
        functools.partial(_attn_ctx_kernel, lambda_init=lambda_init),
        grid=(heads,),
        in_specs=[
            pl.BlockSpec((n_ctx, HEAD_W), lambda h: (cblk, h)),
            pl.BlockSpec((n_ctx, HEAD_W), lambda h: (cblk, heads + h)),
            pl.BlockSpec((HEAD_W, n_ctx), lambda h: (h, cblk)),
            small(lam_vecs), small(sw),
        ],
        out_specs=pl.BlockSpec((n_ctx, HEAD_W), lambda h: (0, h)),
        out_shape=jax.ShapeDtypeStruct((n_ctx, d), BF16),
        compiler_params=_params(("parallel",)),
        name="diff_attn_ctx",
    )(qk, qk, vt, lam_vecs, sw)
    last_q = nq - 1
    kv_blk = lambda i, k: jnp.where(i < nq, k, nk - 1)
    return pl.pallas_call(
        functools.partial(_attn_kernel, lambda_init=lambda_init, nq=nq, nk=nk, qp=min(tq, 256)),
        grid=(heads, nq + 1, nk),
        in_specs=[
            pl.BlockSpec((tq, HEAD_W), lambda h, i, k: (jnp.minimum(i, last_q), h)),
            pl.BlockSpec((tk, HEAD_W), lambda h, i, k: (kv_blk(i, k), heads + h)),
            pl.BlockSpec((HEAD_W, tk), lambda h, i, k: (h, kv_blk(i, k))),
            pl.BlockSpec((n_ctx, HEAD_W), lambda h, i, k: (0, h)),
            small(lam_vecs), small(sw),
        ],
        out_specs=pl.BlockSpec((tq, HEAD_W), lambda h, i, k: (i, h)),
        out_shape=jax.ShapeDtypeStruct(((nq + 1) * tq, d), BF16),
        scratch_shapes=[pltpu.VMEM((2, 1, tq), F32), pltpu.VMEM((2, 1, tq), F32),
                        pltpu.VMEM((2, HEAD_W, tq), F32)],
        compiler_params=_params(("parallel", "parallel", "arbitrary")),
        name="diff_attn",
    )(qk, qk, vt, o_ctx, lam_vecs, sw)


def _moe_kernel(elo_ref, ehi_ref, src_ref, srcn_ref, dst_ref, x_hbm, w13l_ref, w13h_ref, w2l_ref, w2h_ref,
                y_hbm, xbuf, ybuf, gsem, ssem, *, ts, d, n_tiles):
    del elo_ref, ehi_ref
    i = pl.program_id(0)
    slot = i % 2
    nslot = 1 - slot

    def start_gather(idx_ref, sl):
        for r in range(ts):
            pltpu.make_async_copy(x_hbm.at[pl.ds(idx_ref[0, r], 1)], xbuf.at[sl, pl.ds(r, 1)],
                                  gsem.at[sl]).start(priority=r % 2)

    def wait_gather(sl):
        pltpu.make_async_copy(x_hbm.at[pl.ds(0, ts)], xbuf.at[sl], gsem.at[sl]).wait()

    def wait_scatter(sl):
        pltpu.make_async_copy(ybuf.at[sl], y_hbm.at[pl.ds(0, ts)], ssem.at[sl]).wait()

    @pl.when(i == 0)
    def _():
        start_gather(src_ref, 0)

    wait_gather(slot)

    @pl.when(i >= 2)
    def _():
        wait_scatter(slot)

    start_gather(srcn_ref, nslot)

    xin = xbuf[slot]
    hv = xin[:, :d].astype(BF16)
    g_lower = xin[:, d + 1:d + 2]
    g_higher = xin[:, d + 2:d + 3]
    f = w2l_ref.shape[0]

    def hidden(w13_ref, gate):
        au = jnp.dot(hv, w13_ref[...], preferred_element_type=F32)
        return ((_silu(au[:, :f]) * au[:, f:]) * gate).astype(BF16)

    ybuf[slot] = (jnp.dot(hidden(w13l_ref, g_lower), w2l_ref[...], preferred_element_type=F32)
                  + jnp.dot(hidden(w13h_ref, g_higher), w2h_ref[...], preferred_element_type=F32))

    for r in range(ts):
        pltpu.make_async_copy(ybuf.at[slot, pl.ds(r, 1)], y_hbm.at[pl.ds(dst_ref[0, r], 1)],
                              ssem.at[slot]).start(priority=r % 2)

    @pl.when(i == n_tiles - 1)
    def _():
        wait_scatter(slot)
        wait_scatter(nslot)
        wait_gather(nslot)


def _w13_kernel(w1_ref, w3_ref, o_ref):
    f = w1_ref.shape[-1]
    o_ref[:, :f] = w1_ref[...].astype(BF16)
    o_ref[:, f:] = w3_ref[...].astype(BF16)


def _w13_call(w1, w3):
    depth, n_e, d, f = w1.shape
    tr = _pick(d, (2048, 1024, 512, 256))
    spec_in = pl.BlockSpec((None, None, tr, f), lambda l, e, r: (l, e, r, 0))
    return pl.pallas_call(
        _w13_kernel,
        grid=(depth, n_e, d // tr),
        in_specs=[spec_in, spec_in],
        out_specs=pl.BlockSpec((None, None, tr, 2 * f), lambda l, e, r: (l, e, r, 0)),
        out_shape=jax.ShapeDtypeStruct((depth, n_e, d, 2 * f), BF16),
        compiler_params=_params(("parallel", "parallel", "parallel")),
        name="w13_pack",
    )(w1, w3)


def _moe_plan(info, *, ts, n_pad_rows):
    t = info.shape[0]
    n_buckets = N_EXPERT_GROUPS * PAIRS_PER_GROUP
    bucket = info[:, 0].astype(jnp.int32)
    onehot = (bucket[:, None] == jnp.arange(n_buckets, dtype=jnp.int32)[None, :]).astype(jnp.int32)
    csum = jnp.cumsum(onehot, axis=0)
    rank = jnp.sum(csum * onehot, axis=1) - 1
    counts = csum[-1]
    padded = ((counts + ts - 1) // ts) * ts
    ends = jnp.cumsum(padded)
    starts = ends - padded
    pos = jnp.sum(starts[None, :] * onehot, axis=1) + rank
    tok = jnp.arange(t, dtype=jnp.int32)
    slot_tok = jnp.full((n_pad_rows,), -1, jnp.int32).at[pos].set(tok)
    valid = slot_tok >= 0
    src = jnp.maximum(slot_tok, 0)
    pad_rank = jnp.cumsum(1 - valid.astype(jnp.int32)) - 1
    dst = jnp.where(valid, slot_tok, t + pad_rank).astype(jnp.int32)
    n_tiles = n_pad_rows // ts
    tile_bucket = jnp.sum((jnp.arange(n_tiles, dtype=jnp.int32)[:, None] * ts >= ends[None, :]).astype(jnp.int32),
                          axis=1)
    tile_bucket = jnp.minimum(tile_bucket, n_buckets - 1)
    pair = tile_bucket % PAIRS_PER_GROUP
    pair_lo = jnp.array([0, 0, 0, 1, 1, 2], jnp.int32)[pair]
    pair_hi = jnp.array([1, 2, 3, 2, 3, 3], jnp.int32)[pair]
    base = (tile_bucket // PAIRS_PER_GROUP) * EXPERTS_PER_GROUP
    return (src.reshape(n_tiles, 1, ts), dst.reshape(n_tiles, 1, ts),
            (base + pair_lo).astype(jnp.int32), (base + pair_hi).astype(jnp.int32))


def _moe_call(hx, info, w13, w2, layer):
    t, dx = hx.shape
    d = dx - LANES
    f = w2.shape[2]
    ts = MOE_TILE
    n_pad_rows = t + N_EXPERT_GROUPS * PAIRS_PER_GROUP * ts
    n_tiles = n_pad_rows // ts
    src, dst, e_lo, e_hi = _moe_plan(info, ts=ts, n_pad_rows=n_pad_rows)
    smem_tile = lambda fn: pl.BlockSpec((None, 1, ts), fn, memory_space=pltpu.SMEM)
    w_in = lambda pick: pl.BlockSpec((None, None, d, 2 * f), lambda i, lo, hi: (layer, pick(lo, hi)[i], 0, 0))
    w_out = lambda pick: pl.BlockSpec((None, None, f, d), lambda i, lo, hi: (layer, pick(lo, hi)[i], 0, 0))
    lower = lambda lo, hi: lo
    higher = lambda lo, hi: hi
    grid_spec = pltpu.PrefetchScalarGridSpec(
        num_scalar_prefetch=2,
        grid=(n_tiles,),
        in_specs=[
            smem_tile(lambda i, lo, hi: (i, 0, 0)),
            smem_tile(lambda i, lo, hi: (jnp.minimum(i + 1, n_tiles - 1), 0, 0)),
            smem_tile(lambda i, lo, hi: (i, 0, 0)),
            pl.BlockSpec(memory_space=pl.ANY),
            w_in(lower), w_in(higher), w_out(lower), w_out(higher),
        ],
        out_specs=pl.BlockSpec(memory_space=pl.ANY),
        scratch_shapes=[pltpu.VMEM((2, ts, dx), F32), pltpu.VMEM((2, ts, d), F32),
                        pltpu.SemaphoreType.DMA((2,)), pltpu.SemaphoreType.DMA((2,))],
    )
    return pl.pallas_call(
        functools.partial(_moe_kernel, ts=ts, d=d, n_tiles=n_tiles),
        grid_spec=grid_spec,
        out_shape=jax.ShapeDtypeStruct((n_pad_rows, d), F32),
        compiler_params=_params(("arbitrary",)),
        name="moe_routed",
    )(e_lo, e_hi, src, src, dst, hx, w13, w13, w2, w2)


def _final_kernel(x_ref, y_ref, gmod_ref, fw_ref, o_ref):
    xv = x_ref[...] + gmod_ref[5:6, :] * y_ref[...]
    r = lax.rsqrt(jnp.mean(xv * xv, axis=-1, keepdims=True) + RMS_EPS)
    o_ref[...] = (xv * r) * fw_ref[...]


def _final_call(x, y, gmod, fw, *, n_lat):
    _, d = x.shape
    tm = 256
    row = pl.BlockSpec((tm, d), lambda i: (i, 0))
    full = lambda a: pl.BlockSpec(a.shape, lambda i: (0,) * a.ndim)
    return pl.pallas_call(
        _final_kernel,
        grid=(n_lat // tm,),
        in_specs=[row, row, full(gmod), full(fw)],
        out_specs=row,
        out_shape=jax.ShapeDtypeStruct((n_lat, d), F32),
        compiler_params=_params(("parallel",)),
        name="final_norm",
    )(x, y, gmod, fw)


def _rope_tables(n_lat, n_ctx):
    axis_dim = DIFF_HEAD_DIM // 2
    quarter = axis_dim // 2
    rows = n_lat // GRID_W
    row = jnp.broadcast_to(jnp.arange(rows, dtype=F32)[:, None], (rows, GRID_W)).reshape(-1)
    col = jnp.broadcast_to(jnp.arange(GRID_W, dtype=F32)[None, :], (rows, GRID_W)).reshape(-1)
    inv_freq = ROPE_THETA ** (-jnp.arange(0, axis_dim, 2, dtype=F32) / axis_dim)
    ang_r = row[:, None] * inv_freq
    ang_c = col[:, None] * inv_freq
    zeros = jnp.zeros((n_lat, quarter), F32)
    c = jnp.concatenate([jnp.cos(ang_r), jnp.cos(ang_r), jnp.cos(ang_c), jnp.cos(ang_c)], axis=1)
    sa = jnp.concatenate([-jnp.sin(ang_r), zeros, -jnp.sin(ang_c), zeros], axis=1)
    sb = jnp.concatenate([zeros, jnp.sin(ang_r), zeros, jnp.sin(ang_c)], axis=1)
    pad = lambda a, v: jnp.concatenate([a, jnp.full((n_ctx, DIFF_HEAD_DIM), v, F32)], axis=0)
    return pad(c, 1.0), pad(sa, 0.0), pad(sb, 0.0)


def _pack_mod(m, d):
    z = jnp.zeros((MOD_CTX_ROW - N_MOD, d), F32)
    return jnp.concatenate([m[0].reshape(N_MOD, d), z, m[1].reshape(N_MOD, d), z], axis=0)


def kernel(x, c, ctx, c_ctx, ada_w, ada_b, norm_w, pool_w, pool_b, pool_scale, attn_w_qkv, attn_w_o,
           attn_lambda, attn_subln_w, router_group_w, router_expert_w, expert_w1, expert_w3, expert_w2,
           final_norm_w):
    _, n_lat, d = x.shape
    n_ctx = ctx.shape[1]
    depth = ada_w.shape[0]
    assert x.shape[0] == 1 and n_lat % n_ctx == 0 and n_ctx % 256 == 0 and d % (4 * LANES) == 0

    mods = _adaln(c, c_ctx, ada_w, ada_b)
    mod = [_pack_mod(mods[i], d) for i in range(depth)]

    wr = jnp.concatenate([router_group_w, router_expert_w], axis=-1)
    wr = jnp.pad(wr, ((0, 0), (0, 0), (0, LANES - wr.shape[-1])))
    wr_hi = wr.astype(BF16)
    wr_lo = (wr - wr_hi.astype(F32)).astype(BF16)
    wr2 = jnp.concatenate([wr_hi, wr_lo], axis=-1)

    pool_w16 = pool_w.astype(BF16)
    wqkv16 = attn_w_qkv.astype(BF16)
    wo16 = attn_w_o.astype(BF16)
    w13_16 = _w13_call(expert_w1, expert_w3)
    w2_16 = expert_w2.astype(BF16)
    rope_c, rope_sa, rope_sb = _rope_tables(n_lat, n_ctx)

    xs, y = x[0], None
    for i in range(depth):
        jm = i // 2
        gprev = mod[i - 1] if i > 0 else None
        if i % 2 == 0:
            xs, h2, gates = _pool_call(xs, y, gprev, mod[i], norm_w[i], pool_w16, pool_b[jm],
                                       pool_scale[jm].reshape(1, d), wr2, n_lat=n_lat, layer=i, mixer=jm,
                                       ctx=ctx[0] if i == 0 else None)
        else:
            lambda_init = 0.8 - 0.6 * math.exp(-0.3 * i)
            xs, h, _ = _norm_call(xs, y, gprev, mod[i], norm_w[i, 0:1], None, k_shift=0, n_lat=n_lat)
            qk, vt = _qkv_call(h, wqkv16, jm, rope_c, rope_sa, rope_sb)
            o = _attn_call(qk, vt, attn_lambda[jm], attn_subln_w[jm], lambda_init=lambda_init,
                           n_lat=n_lat, n_ctx=n_ctx)
            xs = _oproj_call(o, wo16, jm, xs, mod[i], n_lat=n_lat)
            _, h2, gates = _norm_call(xs, None, None, mod[i], norm_w[i, 1:2], wr2, k_shift=3, n_lat=n_lat,
                                      layer=i)
        y = _moe_call(h2, gates, w13_16, w2_16, i)
    out = _final_call(xs, y, mod[depth - 1], final_norm_w.reshape(1, d), n_lat=n_lat)
    return out[None]
```

```python
import functools
import math

import jax
import jax.numpy as jnp
from jax import lax
from jax.experimental import pallas as pl
from jax.experimental.pallas import tpu as pltpu

F32 = jnp.float32
BF16 = jnp.bfloat16

GRID_W = 64
RMS_EPS = 1e-6
ROPE_THETA = 10000.0
POOL_WINDOWS = (2, 4, 8, 16)
N_MOD = 6
N_EXPERT_GROUPS = 4
EXPERTS_PER_GROUP = 4
N_EXPERTS = N_EXPERT_GROUPS * EXPERTS_PER_GROUP
DIFF_HEAD_DIM = 128
HEAD_W = 2 * DIFF_HEAD_DIM
LANES = 128
SUBLANES = 8
MXU_DIM = 256
MOD_CTX_ROW = 8
ROUTER_COL0 = N_EXPERT_GROUPS
PAIRS_PER_GROUP = 6
MOE_TILE = 128
ADALN_COLS = 2048
VMEM_LIMIT = 56 * 1024 * 1024
NEG_BIG = -1e30
LOG2E = 1.4426950408889634


def _pick(n, candidates):
    for c in candidates:
        if c <= n and n % c == 0:
            return c
    return n


def _params(semantics):
    return pltpu.CompilerParams(dimension_semantics=semantics, vmem_limit_bytes=VMEM_LIMIT)


def _layer_spec(stacked, layer):
    return pl.BlockSpec((None,) + stacked.shape[1:], lambda *_: (layer,) + (0,) * (stacked.ndim - 1))


def _silu(v):
    return v * jax.nn.sigmoid(v)


def _adaln_kernel(c_ref, cc_ref, w_ref, b_ref, o_ref, acc_ref, s_ref, *, tk, tn, nk):
    k = pl.program_id(2)

    @pl.when(k == 0)
    def _():
        acc_ref[...] = jnp.zeros_like(acc_ref)

    s_ref[0] = _silu(c_ref[...])
    s_ref[1] = _silu(cc_ref[...])

    cw = min(tn, ADALN_COLS)
    reps = cw // LANES
    for c0 in range(0, tn, cw):
        def body(j, carry, c0=c0):
            a0, a1 = carry
            r = pl.multiple_of(j * SUBLANES, SUBLANES)
            w = w_ref[pl.ds(r, SUBLANES), c0:c0 + cw]
            a0 = a0 + w * jnp.concatenate([s_ref[0, pl.ds(r, SUBLANES), :]] * reps, axis=1)
            a1 = a1 + w * jnp.concatenate([s_ref[1, pl.ds(r, SUBLANES), :]] * reps, axis=1)
            return a0, a1

        z = jnp.zeros((SUBLANES, cw), F32)
        a0, a1 = lax.fori_loop(0, tk // SUBLANES, body, (z, z), unroll=4)
        acc_ref[0, :, c0:c0 + cw] += a0
        acc_ref[1, :, c0:c0 + cw] += a1

    @pl.when(k == nk - 1)
    def _():
        o_ref[0:1, :] = jnp.sum(acc_ref[0], axis=0, keepdims=True) + b_ref[...]
        o_ref[1:2, :] = jnp.sum(acc_ref[1], axis=0, keepdims=True) + b_ref[...]


def _adaln(c, c_ctx, ada_w, ada_b):
    depth, d, n = ada_w.shape
    tk = _pick(d, (512, 256, 128))
    tn = _pick(n, (4096, 2048, 1024, 512, 256, 128))
    nk = d // tk
    c_b = jnp.broadcast_to(c.reshape(d, 1), (d, LANES))
    cc_b = jnp.broadcast_to(c_ctx.reshape(d, 1), (d, LANES))
    return pl.pallas_call(
        functools.partial(_adaln_kernel, tk=tk, tn=tn, nk=nk),
        grid=(depth, n // tn, nk),
        in_specs=[
            pl.BlockSpec((tk, LANES), lambda l, j, k: (k, 0)),
            pl.BlockSpec((tk, LANES), lambda l, j, k: (k, 0)),
            pl.BlockSpec((None, tk, tn), lambda l, j, k: (l, k, j)),
            pl.BlockSpec((None, 1, tn), lambda l, j, k: (l, 0, j)),
        ],
        out_specs=pl.BlockSpec((None, 2, tn), lambda l, j, k: (l, 0, j)),
        out_shape=jax.ShapeDtypeStruct((depth, 2, n), F32),
        scratch_shapes=[pltpu.VMEM((2, SUBLANES, tn), F32), pltpu.VMEM((2, tk, LANES), F32)],
        compiler_params=_params(("parallel", "parallel", "arbitrary")),
        name="adaln",
    )(c_b, cc_b, ada_w, ada_b.reshape(depth, 1, n))


def _mod_row(mod_ref, k, is_ctx):
    return jnp.where(is_ctx, mod_ref[MOD_CTX_ROW + k:MOD_CTX_ROW + k + 1, :], mod_ref[k:k + 1, :])


def _norm_mod(xv, w_row, shift, scale):
    r = lax.rsqrt(jnp.mean(xv * xv, axis=-1, keepdims=True) + RMS_EPS)
    return (xv * r) * (w_row * (1.0 + scale)) + shift


def _router_info(h, h_hi, wr_ref):
    h_lo = (h - h_hi.astype(F32)).astype(BF16)
    r1 = jnp.dot(h_hi, wr_ref[...], preferred_element_type=F32)
    r2 = jnp.dot(h_lo, wr_ref[:, :LANES], preferred_element_type=F32)
    logits = r1[:, :LANES] + r1[:, LANES:] + r2
    col = lax.broadcasted_iota(jnp.int32, logits.shape, 1).astype(F32)
    far = float(4 * LANES)
    is_g = col < float(N_EXPERT_GROUPS)
    mg = jnp.max(jnp.where(is_g, logits, NEG_BIG), axis=1, keepdims=True)
    denom = jnp.sum(jnp.where(is_g, jnp.exp(logits - mg), 0.0), axis=1, keepdims=True)
    pg_top = 1.0 / denom
    g_sel = jnp.min(jnp.where(is_g & (logits == mg), col, far), axis=1, keepdims=True)
    e_lo = float(ROUTER_COL0) + float(EXPERTS_PER_GROUP) * g_sel
    in_grp = (col >= e_lo) & (col < e_lo + float(EXPERTS_PER_GROUP))
    v1 = jnp.max(jnp.where(in_grp, logits, NEG_BIG), axis=1, keepdims=True)
    i1 = jnp.min(jnp.where(in_grp & (logits == v1), col, far), axis=1, keepdims=True)
    rest = in_grp & (col != i1)
    v2 = jnp.max(jnp.where(rest, logits, NEG_BIG), axis=1, keepdims=True)
    i2 = jnp.min(jnp.where(rest & (logits == v2), col, far), axis=1, keepdims=True)
    e2 = jnp.exp(v2 - v1)
    w1 = pg_top / (1.0 + e2)
    w2 = pg_top * (e2 / (1.0 + e2))
    first_lower = i1 < i2
    ja = jnp.minimum(i1, i2) - e_lo
    jb = jnp.maximum(i1, i2) - e_lo
    bucket = g_sel * float(PAIRS_PER_GROUP) + ja * (7.0 - ja) * 0.5 + (jb - ja - 1.0)
    g_lower = jnp.where(first_lower, w1, w2)
    g_higher = jnp.where(first_lower, w2, w1)
    return jnp.where(col == 0.0, bucket, jnp.where(col == 1.0, g_lower, jnp.where(col == 2.0, g_higher, 0.0)))


def _emit_routed_rows(h, wr_ref, hx_ref, info_ref):
    half = h.shape[1] // 2
    h_hi = h.astype(BF16)
    info = _router_info(h, h_hi, wr_ref)
    bits = lax.bitcast_convert_type(h_hi.astype(F32), jnp.uint32)
    word = lax.shift_right_logical(bits[:, :half], jnp.uint32(16)) | bits[:, half:]
    hx_ref[:, :half] = lax.bitcast_convert_type(word, F32)
    hx_ref[:, half:] = info
    info_ref[...] = info


def _norm_kernel(*refs, has_y, router, k_shift, lat_tiles):
    it = iter(refs)
    x_ref = next(it)
    y_ref = next(it) if has_y else None
    gmod_ref = next(it) if has_y else None
    mod_ref = next(it)
    nw_ref = next(it)
    wr_ref = next(it) if router else None
    xo_ref = next(it) if has_y else None
    h_ref = next(it)
    g_ref = next(it) if router else None

    is_ctx = pl.program_id(0) >= lat_tiles
    xv = x_ref[...]
    if has_y:
        xv = xv + _mod_row(gmod_ref, 5, is_ctx) * y_ref[...]
        xo_ref[...] = xv
    h = _norm_mod(xv, nw_ref[...], _mod_row(mod_ref, k_shift, is_ctx), _mod_row(mod_ref, k_shift + 1, is_ctx))
    if router:
        _emit_routed_rows(h, wr_ref, h_ref, g_ref)
    else:
        h_ref[...] = h.astype(BF16)


def _norm_call(x, y, gmod, mod, nw, wr, *, k_shift, n_lat, layer=0):
    t, d = x.shape
    tm = 256
    has_y = y is not None
    router = wr is not None
    row = pl.BlockSpec((tm, d), lambda i: (i, 0))
    full = lambda a: pl.BlockSpec(a.shape, lambda i: (0,) * a.ndim)
    ins, specs = [x], [row]
    if has_y:
        ins += [y, gmod]
        specs += [row, full(gmod)]
    ins += [mod, nw]
    specs += [full(mod), full(nw)]
    if router:
        ins.append(wr)
        specs.append(_layer_spec(wr, layer))
    outs, ospecs = [], []
    if has_y:
        outs.append(jax.ShapeDtypeStruct((t, d), F32))
        ospecs.append(row)
    if router:
        outs += [jax.ShapeDtypeStruct((t, d // 2 + LANES), F32), jax.ShapeDtypeStruct((t, LANES), F32)]
        ospecs += [pl.BlockSpec((tm, d // 2 + LANES), lambda i: (i, 0)), pl.BlockSpec((tm, LANES), lambda i: (i, 0))]
    else:
        outs.append(jax.ShapeDtypeStruct((t, d), BF16))
        ospecs.append(row)
    res = pl.pallas_call(
        functools.partial(_norm_kernel, has_y=has_y, router=router, k_shift=k_shift, lat_tiles=n_lat // tm),
        grid=(t // tm,),
        in_specs=specs,
        out_specs=ospecs,
        out_shape=outs,
        compiler_params=_params(("parallel",)),
        name="norm_mod",
    )(*ins)
    res = list(res)
    xo = res.pop(0) if has_y else x
    h = res.pop(0)
    g = res.pop(0) if router else None
    return xo, h, g


HALO = 8


def _pool_kernel(*refs, has_y, split_in, tp, cg, lat_tiles, n_tiles):
    it = iter(refs)
    x_ref, xp_ref, xn_ref = next(it), next(it), next(it)
    if split_in:
        c_ref, cp_ref, cn_ref = next(it), next(it), next(it)
    if has_y:
        y_ref, yp_ref, yn_ref, gmod_ref = next(it), next(it), next(it), next(it)
    mod_ref, nw_ref, pw_ref, pb_ref, ps_ref, wr_ref = (next(it) for _ in range(6))
    x1_ref, h2_ref, g_ref = next(it), next(it), next(it)
    hext_ref, xin_ref = next(it), next(it)

    j = pl.program_id(0)
    is_ctx = j >= lat_tiles
    first = (j == 0) | (j == lat_tiles)
    last = (j == lat_tiles - 1) | (j == n_tiles - 1)

    nw = nw_ref[0:1, :]
    ws = nw * (1.0 + _mod_row(mod_ref, 1, is_ctx))
    shift = _mod_row(mod_ref, 0, is_ctx)

    def x_in(xr, yr, cr):
        xv = xr[...]
        if split_in:
            xv = jnp.where(is_ctx, cr[...], xv)
        if has_y:
            xv = xv + _mod_row(gmod_ref, 5, is_ctx) * yr[...]
        return xv

    def h_of(xv):
        r = lax.rsqrt(jnp.mean(xv * xv, axis=-1, keepdims=True) + RMS_EPS)
        return (xv * r) * ws + shift

    xm = x_in(x_ref, y_ref if has_y else None, c_ref if split_in else None)
    xin_ref[...] = xm
    hext_ref[HALO:HALO + tp, :] = h_of(xm)
    hp = h_of(x_in(xp_ref, yp_ref if has_y else None, cp_ref if split_in else None))
    hext_ref[0:HALO, :] = jnp.where(first, 0.0, hp)
    hn = h_of(x_in(xn_ref, yn_ref if has_y else None, cn_ref if split_in else None))
    hext_ref[HALO + tp:HALO + tp + HALO, :] = jnp.where(last, 0.0, hn)

    hext_ref[tp + 2 * HALO:, :] = jnp.zeros((hext_ref.shape[0] - tp - 2 * HALO, hext_ref.shape[1]), F32)

    tpos = lax.broadcasted_iota(jnp.int32, (tp, 1), 0).astype(F32)
    lo_lim = jnp.where(first, 0.0, -float(2 * HALO))
    hi_lim = jnp.where(last, float(tp), float(tp + 2 * HALO))
    gate = _mod_row(mod_ref, 2, is_ctx)
    win_t = lax.broadcasted_iota(jnp.int32, (tp, hext_ref.shape[0]), 0)
    win_s = lax.broadcasted_iota(jnp.int32, (tp, hext_ref.shape[0]), 1)
    for g, w in enumerate(POOL_WINDOWS):
        c0 = g * cg
        half = w // 2
        band = jnp.where((win_s >= win_t + (HALO - half)) & (win_s < win_t + (HALO + half)), 1.0, 0.0).astype(BF16)
        hx = hext_ref[:, c0:c0 + cg]
        hx_hi = hx.astype(BF16)
        hx_lo = (hx - hx_hi.astype(F32)).astype(BF16)
        wsum = (jnp.dot(band, hx_hi, preferred_element_type=F32)
                + jnp.dot(band, hx_lo, preferred_element_type=F32))
        cnt = jnp.minimum(tpos + float(half), hi_lim) - jnp.maximum(tpos - float(half), lo_lim)
        pooled = wsum / cnt - hext_ref[HALO:HALO + tp, c0:c0 + cg]
        yg = jnp.dot(pooled.astype(BF16), pw_ref[g], preferred_element_type=F32) + pb_ref[g:g + 1, :]
        yg = yg * ps_ref[:, c0:c0 + cg]
        x1_ref[:, c0:c0 + cg] = xin_ref[:, c0:c0 + cg] + gate[:, c0:c0 + cg] * yg

    x1 = x1_ref[...]
    h2 = _norm_mod(x1, nw_ref[1:2, :], _mod_row(mod_ref, 3, is_ctx), _mod_row(mod_ref, 4, is_ctx))
    _emit_routed_rows(h2, wr_ref, h2_ref, g_ref)


def _pool_call(x, y, gmod, mod, nw2, pw, pb, ps, wr, *, n_lat, layer, mixer, ctx=None):
    d = x.shape[1]
    split_in = ctx is not None
    t = x.shape[0] + (ctx.shape[0] if split_in else 0)
    tp = 128
    cg = d // len(POOL_WINDOWS)
    has_y = y is not None
    n_tiles = t // tp
    lat_tiles = n_lat // tp
    bpt = tp // HALO

    def halo_specs(n_rows, tile0):
        n_t, last_blk = n_rows // tp, n_rows // HALO - 1
        clip = lambda v, hi: jnp.minimum(jnp.maximum(v, 0), hi)
        return [pl.BlockSpec((tp, d), lambda i: (clip(i - tile0, n_t - 1), 0)),
                pl.BlockSpec((HALO, d), lambda i: (clip((i - tile0) * bpt - 1, last_blk), 0)),
                pl.BlockSpec((HALO, d), lambda i: (clip((i - tile0 + 1) * bpt, last_blk), 0))]

    row = pl.BlockSpec((tp, d), lambda i: (i, 0))
    full = lambda a: pl.BlockSpec(a.shape, lambda i: (0,) * a.ndim)
    ins, specs = [x, x, x], halo_specs(x.shape[0], 0)
    if split_in:
        ins += [ctx, ctx, ctx]
        specs += halo_specs(ctx.shape[0], lat_tiles)
    _, prev, nxt = halo_specs(t, 0)
    if has_y:
        ins += [y, y, y, gmod]
        specs += [row, prev, nxt, full(gmod)]
    ins += [mod, nw2, pw, pb, ps, wr]
    specs += [full(mod), full(nw2), _layer_spec(pw, mixer), full(pb), full(ps), _layer_spec(wr, layer)]
    return pl.pallas_call(
        functools.partial(_pool_kernel, has_y=has_y, split_in=split_in, tp=tp, cg=cg, lat_tiles=lat_tiles,
                          n_tiles=n_tiles),
        grid=(n_tiles,),
        in_specs=specs,
        out_specs=[row, pl.BlockSpec((tp, d // 2 + LANES), lambda i: (i, 0)),
                   pl.BlockSpec((tp, LANES), lambda i: (i, 0))],
        out_shape=[jax.ShapeDtypeStruct((t, d), F32), jax.ShapeDtypeStruct((t, d // 2 + LANES), F32),
                   jax.ShapeDtypeStruct((t, LANES), F32)],
        scratch_shapes=[pltpu.VMEM((-(-(tp + 2 * HALO) // MXU_DIM) * MXU_DIM, d), F32), pltpu.VMEM((tp, d), F32)],
        compiler_params=_params(("parallel",)),
        name="pool_layer",
    )(*ins)


def _qk_kernel(a_ref, w_ref, c_ref, sa_ref, sb_ref, o_ref, *, d, tn, qscale):
    acc = jnp.dot(a_ref[...], w_ref[...], preferred_element_type=F32)
    f = jnp.where(pl.program_id(1) < d // tn, qscale, 1.0)
    cc = c_ref[...] * f
    sa = sa_ref[...] * f
    sb = sb_ref[...] * f
    for b in range(tn // LANES):
        blk = acc[:, b * LANES:(b + 1) * LANES]
        rot = blk * cc + pltpu.roll(blk, 96, 1) * sa + pltpu.roll(blk, 32, 1) * sb
        o_ref[:, b * LANES:(b + 1) * LANES] = rot.astype(BF16)


def _vt_kernel(a_ref, w_ref, o_ref, acc_ref):
    acc_ref[...] = jnp.dot(a_ref[...], w_ref[...], preferred_element_type=F32)
    o_ref[...] = acc_ref[...].T.astype(BF16)


def _qkv_call(h, w, mixer, rope_c, rope_sa, rope_sb):
    t, d = h.shape
    tm = _pick(t, (640, 512, 256, 128))
    tn = _pick(d, (1024, 512, 256))
    qscale = (DIFF_HEAD_DIM ** -0.5) * LOG2E
    tab = pl.BlockSpec((tm, LANES), lambda i, j: (i, 0))
    a_spec = pl.BlockSpec((tm, d), lambda i, j: (i, 0))
    qk = pl.pallas_call(
        functools.partial(_qk_kernel, d=d, tn=tn, qscale=qscale),
        grid=(t // tm, (2 * d) // tn),
        in_specs=[a_spec, pl.BlockSpec((None, d, tn), lambda i, j: (mixer, 0, j)), tab, tab, tab],
        out_specs=pl.BlockSpec((tm, tn), lambda i, j: (i, j)),
        out_shape=jax.ShapeDtypeStruct((t, 2 * d), BF16),
        compiler_params=_params(("parallel", "arbitrary")),
        name="qk_proj",
    )(h, w, rope_c, rope_sa, rope_sb)
    v_col0 = (2 * d) // tn
    vt = pl.pallas_call(
        _vt_kernel,
        grid=(t // tm, d // tn),
        in_specs=[a_spec, pl.BlockSpec((None, d, tn), lambda i, j: (mixer, 0, v_col0 + j))],
        out_specs=pl.BlockSpec((tn, tm), lambda i, j: (j, i)),
        out_shape=jax.ShapeDtypeStruct((d, t), BF16),
        scratch_shapes=[pltpu.VMEM((tm, tn), F32)],
        compiler_params=_params(("parallel", "arbitrary")),
        name="v_proj_t",
    )(h, w)
    return qk, vt


def _oproj_kernel(a_ref, w_ref, x_ref, mod_ref, o_ref, *, tm, n_lat):
    acc = jnp.dot(a_ref[...], w_ref[...], preferred_element_type=F32)
    rows = pl.program_id(0) * tm + lax.broadcasted_iota(jnp.int32, (tm, 1), 0)
    gate = jnp.where(rows >= n_lat, mod_ref[MOD_CTX_ROW + 2:MOD_CTX_ROW + 3, :], mod_ref[2:3, :])
    o_ref[...] = x_ref[...] + gate * acc


def _oproj_call(o, w, mixer, x, mod, *, n_lat):
    t, d = x.shape
    tm = _pick(t, (640, 512, 256, 128))
    tn = _pick(d, (1024, 512, 256))
    return pl.pallas_call(
        functools.partial(_oproj_kernel, tm=tm, n_lat=n_lat),
        grid=(t // tm, d // tn),
        in_specs=[pl.BlockSpec((tm, d), lambda i, j: (i, 0)),
                  pl.BlockSpec((None, d, tn), lambda i, j: (mixer, 0, j)),
                  pl.BlockSpec((tm, tn), lambda i, j: (i, j)), pl.BlockSpec((16, tn), lambda i, j: (0, j))],
        out_specs=pl.BlockSpec((tm, tn), lambda i, j: (i, j)),
        out_shape=jax.ShapeDtypeStruct((t, d), F32),
        compiler_params=_params(("parallel", "arbitrary")),
        name="attn_out_proj",
    )(o, w, x, mod)


def _diff_lambda(lam_ref, lambda_init):
    lv = lam_ref[...]
    a = jnp.sum(lv[0:1, :] * lv[1:2, :], axis=1, keepdims=True)
    b = jnp.sum(lv[2:3, :] * lv[3:4, :], axis=1, keepdims=True)
    return jnp.exp(a) - jnp.exp(b) + lambda_init


def _attn_scores_t(q_ref, k_ref, c):
    qc = q_ref[:, c * DIFF_HEAD_DIM:(c + 1) * DIFF_HEAD_DIM]
    kc = k_ref[:, c * DIFF_HEAD_DIM:(c + 1) * DIFF_HEAD_DIM]
    return lax.dot_general(kc, qc, (((1,), (1,)), ((), ())), preferred_element_type=F32)


def _attn_finish(ot, sw_ref, lambda_init):
    r = lax.rsqrt(jnp.mean(ot * ot, axis=0, keepdims=True) + RMS_EPS)
    return (((ot * r).T * sw_ref[...]) * (1.0 - lambda_init)).astype(BF16)


def _attn_kernel(q_ref, k_ref, vt_ref, octx_ref, lam_ref, sw_ref, o_ref, m_ref, l_ref, acc_ref,
                 *, lambda_init, nq, nk, qp):
    qi = pl.program_id(1)
    ki = pl.program_id(2)

    @pl.when(qi < nq)
    def _():
        _attn_step(q_ref, k_ref, vt_ref, lam_ref, sw_ref, o_ref, m_ref, l_ref, acc_ref, ki,
                   lambda_init=lambda_init, nk=nk, qp=qp)

    @pl.when((qi == nq) & (ki == nk - 1))
    def _():
        n_ctx = octx_ref.shape[0]
        o_ref[0:n_ctx, :] = octx_ref[...]
        o_ref[n_ctx:, :] = jnp.zeros((o_ref.shape[0] - n_ctx, o_ref.shape[1]), o_ref.dtype)


def _attn_step(q_ref, k_ref, vt_ref, lam_ref, sw_ref, o_ref, m_ref, l_ref, acc_ref, ki, *, lambda_init, nk, qp):
    @pl.when(ki == 0)
    def _():
        m_ref[...] = jnp.full(m_ref.shape, NEG_BIG, F32)
        l_ref[...] = jnp.zeros_like(l_ref)
        acc_ref[...] = jnp.zeros_like(acc_ref)

    tq = q_ref.shape[0]
    chains = [(c, j) for j in range(tq // qp) for c in range(2)]

    def scores(c, j):
        qc = q_ref[j * qp:(j + 1) * qp, c * DIFF_HEAD_DIM:(c + 1) * DIFF_HEAD_DIM]
        kc = k_ref[:, c * DIFF_HEAD_DIM:(c + 1) * DIFF_HEAD_DIM]
        return lax.dot_general(kc, qc, (((1,), (1,)), ((), ())), preferred_element_type=F32)

    s_next = scores(*chains[0])
    for n, (c, j) in enumerate(chains):
        s = s_next
        if n + 1 < len(chains):
            s_next = scores(*chains[n + 1])
        cols = slice(j * qp, (j + 1) * qp)
        m_prev = m_ref[c, :, cols]
        m_new = jnp.maximum(m_prev, jnp.max(s, axis=0, keepdims=True))
        alpha = jnp.exp2(m_prev - m_new)
        p = jnp.exp2(s - m_new)
        l_ref[c, :, cols] = alpha * l_ref[c, :, cols] + jnp.sum(p, axis=0, keepdims=True)
        acc_ref[c, :, cols] = alpha * acc_ref[c, :, cols] + jnp.dot(
            vt_ref[...], p.astype(BF16), preferred_element_type=F32)
        m_ref[c, :, cols] = m_new

    @pl.when(ki == nk - 1)
    def _():
        lam = _diff_lambda(lam_ref, lambda_init)
        ot = acc_ref[0] / l_ref[0] - lam * (acc_ref[1] / l_ref[1])
        o_ref[...] = _attn_finish(ot, sw_ref, lambda_init)


def _attn_ctx_kernel(q_ref, k_ref, vt_ref, lam_ref, sw_ref, o_ref, *, lambda_init):
    lam = _diff_lambda(lam_ref, lambda_init)
    outs = []
    for c in range(2):
        s = _attn_scores_t(q_ref, k_ref, c)
        p = jnp.exp2(s - jnp.max(s, axis=0, keepdims=True))
        l = jnp.sum(p, axis=0, keepdims=True)
        outs.append(jnp.dot(vt_ref[...], p.astype(BF16), preferred_element_type=F32) / l)
    o_ref[...] = _attn_finish(outs[0] - lam * outs[1], sw_ref, lambda_init)


def _attn_call(qk, vt, lam_vecs, subln_w, *, lambda_init, n_lat, n_ctx):
    d, t = vt.shape
    heads = d // HEAD_W
    tq = _pick(n_lat, (4096, 2048, 1024, 512, 256, 128))
    tk = _pick(t, (3328, 1280, 640, 256, 128))
    nq = n_lat // tq
    nk = t // tk
    cblk = n_lat // n_ctx
    assert n_ctx <= tq
    sw = subln_w.reshape(1, HEAD_W)
    small = lambda a: pl.BlockSpec(a.shape, lambda *_: (0,) * a.ndim)
    o_ctx = pl.pallas_call(
        functools.partial(_attn_ctx_kernel, lambda_init=lambda_init),
        grid=(heads,),
        in_specs=[
            pl.BlockSpec((n_ctx, HEAD_W), lambda h: (cblk, h)),
            pl.BlockSpec((n_ctx, HEAD_W), lambda h: (cblk, heads + h)),
            pl.BlockSpec((HEAD_W, n_ctx), lambda h: (h, cblk)),
            small(lam_vecs), small(sw),
        ],
        out_specs=pl.BlockSpec((n_ctx, HEAD_W), lambda h: (0, h)),
        out_shape=jax.ShapeDtypeStruct((n_ctx, d), BF16),
        compiler_params=_params(("parallel",)),
        name="diff_attn_ctx",
    )(qk, qk, vt, lam_vecs, sw)
    last_q = nq - 1
    kv_blk = lambda i, k: jnp.where(i < nq, k, nk - 1)
    return pl.pallas_call(
        functools.partial(_attn_kernel, lambda_init=lambda_init, nq=nq, nk=nk, qp=min(tq, 256)),
        grid=(heads, nq + 1, nk),
        in_specs=[
            pl.BlockSpec((tq, HEAD_W), lambda h, i, k: (jnp.minimum(i, last_q), h)),
            pl.BlockSpec((tk, HEAD_W), lambda h, i, k: (kv_blk(i, k), heads + h)),
            pl.BlockSpec((HEAD_W, tk), lambda h, i, k: (h, kv_blk(i, k))),
            pl.BlockSpec((n_ctx, HEAD_W), lambda h, i, k: (0, h)),
            small(lam_vecs), small(sw),
        ],
        out_specs=pl.BlockSpec((tq, HEAD_W), lambda h, i, k: (i, h)),
        out_shape=jax.ShapeDtypeStruct(((nq + 1) * tq, d), BF16),
        scratch_shapes=[pltpu.VMEM((2, 1, tq), F32), pltpu.VMEM((2, 1, tq), F32),
                        pltpu.VMEM((2, HEAD_W, tq), F32)],
        compiler_params=_params(("parallel", "parallel", "arbitrary")),
        name="diff_attn",
    )(qk, qk, vt, o_ctx, lam_vecs, sw)


def _moe_kernel(elo_ref, ehi_ref, src_ref, srcn_ref, dst_ref, x_hbm, w13l_ref, w13h_ref, w2l_ref, w2h_ref,
                y_hbm, xbuf, ybuf, gsem, ssem, *, ts, d, n_tiles):
    del elo_ref, ehi_ref
    i = pl.program_id(0)
    slot = i % 2
    nslot = 1 - slot

    def start_gather(idx_ref, sl):
        for r in range(ts):
            pltpu.make_async_copy(x_hbm.at[pl.ds(idx_ref[0, r], 1)], xbuf.at[sl, pl.ds(r, 1)],
                                  gsem.at[sl]).start(priority=r % 2)

    def wait_gather(sl):
        pltpu.make_async_copy(x_hbm.at[pl.ds(0, ts)], xbuf.at[sl], gsem.at[sl]).wait()

    def wait_scatter(sl):
        pltpu.make_async_copy(ybuf.at[sl], y_hbm.at[pl.ds(0, ts)], ssem.at[sl]).wait()

    @pl.when(i == 0)
    def _():
        start_gather(src_ref, 0)

    wait_gather(slot)

    @pl.when(i >= 2)
    def _():
        wait_scatter(slot)

    start_gather(srcn_ref, nslot)

    xin = xbuf[slot]
    half = d // 2
    word = lax.bitcast_convert_type(xin[:, :half], jnp.uint32)
    h_a = lax.bitcast_convert_type(lax.shift_left(word, jnp.uint32(16)), F32).astype(BF16)
    h_b = lax.bitcast_convert_type(word & jnp.uint32(0xFFFF0000), F32).astype(BF16)
    g_lower = xin[:, half + 1:half + 2]
    g_higher = xin[:, half + 2:half + 3]
    f = w2l_ref.shape[0]

    def hidden(w13_ref, gate):
        au = (jnp.dot(h_a, w13_ref[:half, :], preferred_element_type=F32)
              + jnp.dot(h_b, w13_ref[half:, :], preferred_element_type=F32))
        return ((_silu(au[:, :f]) * au[:, f:]) * gate).astype(BF16)

    ybuf[slot] = (jnp.dot(hidden(w13l_ref, g_lower), w2l_ref[...], preferred_element_type=F32)
                  + jnp.dot(hidden(w13h_ref, g_higher), w2h_ref[...], preferred_element_type=F32))

    for r in range(ts):
        pltpu.make_async_copy(ybuf.at[slot, pl.ds(r, 1)], y_hbm.at[pl.ds(dst_ref[0, r], 1)],
                              ssem.at[slot]).start(priority=r % 2)

    @pl.when(i == n_tiles - 1)
    def _():
        wait_scatter(slot)
        wait_scatter(nslot)
        wait_gather(nslot)


def _w13_kernel(w1_ref, w3_ref, o_ref):
    f = w1_ref.shape[-1]
    o_ref[:, :f] = w1_ref[...].astype(BF16)
    o_ref[:, f:] = w3_ref[...].astype(BF16)


def _w13_call(w1, w3):
    depth, n_e, d, f = w1.shape
    tr = _pick(d, (2048, 1024, 512, 256))
    spec_in = pl.BlockSpec((None, None, tr, f), lambda l, e, r: (l, e, r, 0))
    return pl.pallas_call(
        _w13_kernel,
        grid=(depth, n_e, d // tr),
        in_specs=[spec_in, spec_in],
        out_specs=pl.BlockSpec((None, None, tr, 2 * f), lambda l, e, r: (l, e, r, 0)),
        out_shape=jax.ShapeDtypeStruct((depth, n_e, d, 2 * f), BF16),
        compiler_params=_params(("parallel", "parallel", "parallel")),
        name="w13_pack",
    )(w1, w3)


def _moe_plan(info, *, ts, n_pad_rows):
    t = info.shape[0]
    n_buckets = N_EXPERT_GROUPS * PAIRS_PER_GROUP
    bucket = info[:, 0].astype(jnp.int32)
    onehot = (bucket[:, None] == jnp.arange(n_buckets, dtype=jnp.int32)[None, :]).astype(jnp.int32)
    csum = jnp.cumsum(onehot, axis=0)
    rank = jnp.sum(csum * onehot, axis=1) - 1
    counts = csum[-1]
    padded = ((counts + ts - 1) // ts) * ts
    ends = jnp.cumsum(padded)
    starts = ends - padded
    pos = jnp.sum(starts[None, :] * onehot, axis=1) + rank
    tok = jnp.arange(t, dtype=jnp.int32)
    slot_tok = jnp.full((n_pad_rows,), -1, jnp.int32).at[pos].set(tok)
    valid = slot_tok >= 0
    src = jnp.maximum(slot_tok, 0)
    pad_rank = jnp.cumsum(1 - valid.astype(jnp.int32)) - 1
    dst = jnp.where(valid, slot_tok, t + pad_rank).astype(jnp.int32)
    n_tiles = n_pad_rows // ts
    tile_bucket = jnp.sum((jnp.arange(n_tiles, dtype=jnp.int32)[:, None] * ts >= ends[None, :]).astype(jnp.int32),
                          axis=1)
    tile_bucket = jnp.minimum(tile_bucket, n_buckets - 1)
    pair = tile_bucket % PAIRS_PER_GROUP
    pair_lo = jnp.array([0, 0, 0, 1, 1, 2], jnp.int32)[pair]
    pair_hi = jnp.array([1, 2, 3, 2, 3, 3], jnp.int32)[pair]
    base = (tile_bucket // PAIRS_PER_GROUP) * EXPERTS_PER_GROUP
    return (src.reshape(n_tiles, 1, ts), dst.reshape(n_tiles, 1, ts),
            (base + pair_lo).astype(jnp.int32), (base + pair_hi).astype(jnp.int32))


def _moe_call(hx, info, w13, w2, layer):
    t, dx = hx.shape
    d = 2 * (dx - LANES)
    f = w2.shape[2]
    ts = MOE_TILE
    n_pad_rows = t + N_EXPERT_GROUPS * PAIRS_PER_GROUP * ts
    n_tiles = n_pad_rows // ts
    src, dst, e_lo, e_hi = _moe_plan(info, ts=ts, n_pad_rows=n_pad_rows)
    smem_tile = lambda fn: pl.BlockSpec((None, 1, ts), fn, memory_space=pltpu.SMEM)
    w_in = lambda pick: pl.BlockSpec((None, None, d, 2 * f), lambda i, lo, hi: (layer, pick(lo, hi)[i], 0, 0))
    w_out = lambda pick: pl.BlockSpec((None, None, f, d), lambda i, lo, hi: (layer, pick(lo, hi)[i], 0, 0))
    lower = lambda lo, hi: lo
    higher = lambda lo, hi: hi
    grid_spec = pltpu.PrefetchScalarGridSpec(
        num_scalar_prefetch=2,
        grid=(n_tiles,),
        in_specs=[
            smem_tile(lambda i, lo, hi: (i, 0, 0)),
            smem_tile(lambda i, lo, hi: (jnp.minimum(i + 1, n_tiles - 1), 0, 0)),
            smem_tile(lambda i, lo, hi: (i, 0, 0)),
            pl.BlockSpec(memory_space=pl.ANY),
            w_in(lower), w_in(higher), w_out(lower), w_out(higher),
        ],
        out_specs=pl.BlockSpec(memory_space=pl.ANY),
        scratch_shapes=[pltpu.VMEM((2, ts, dx), F32), pltpu.VMEM((2, ts, d), F32),
                        pltpu.SemaphoreType.DMA((2,)), pltpu.SemaphoreType.DMA((2,))],
    )
    return pl.pallas_call(
        functools.partial(_moe_kernel, ts=ts, d=d, n_tiles=n_tiles),
        grid_spec=grid_spec,
        out_shape=jax.ShapeDtypeStruct((n_pad_rows, d), F32),
        compiler_params=_params(("arbitrary",)),
        name="moe_routed",
    )(e_lo, e_hi, src, src, dst, hx, w13, w13, w2, w2)


def _final_kernel(x_ref, y_ref, gmod_ref, fw_ref, o_ref):
    xv = x_ref[...] + gmod_ref[5:6, :] * y_ref[...]
    r = lax.rsqrt(jnp.mean(xv * xv, axis=-1, keepdims=True) + RMS_EPS)
    o_ref[...] = (xv * r) * fw_ref[...]


def _final_call(x, y, gmod, fw, *, n_lat):
    _, d = x.shape
    tm = 256
    row = pl.BlockSpec((tm, d), lambda i: (i, 0))
    full = lambda a: pl.BlockSpec(a.shape, lambda i: (0,) * a.ndim)
    return pl.pallas_call(
        _final_kernel,
        grid=(n_lat // tm,),
        in_specs=[row, row, full(gmod), full(fw)],
        out_specs=row,
        out_shape=jax.ShapeDtypeStruct((n_lat, d), F32),
        compiler_params=_params(("parallel",)),
        name="final_norm",
    )(x, y, gmod, fw)


def _rope_tables(n_lat, n_ctx):
    axis_dim = DIFF_HEAD_DIM // 2
    quarter = axis_dim // 2
    rows = n_lat // GRID_W
    row = jnp.broadcast_to(jnp.arange(rows, dtype=F32)[:, None], (rows, GRID_W)).reshape(-1)
    col = jnp.broadcast_to(jnp.arange(GRID_W, dtype=F32)[None, :], (rows, GRID_W)).reshape(-1)
    inv_freq = ROPE_THETA ** (-jnp.arange(0, axis_dim, 2, dtype=F32) / axis_dim)
    ang_r = row[:, None] * inv_freq
    ang_c = col[:, None] * inv_freq
    zeros = jnp.zeros((n_lat, quarter), F32)
    c = jnp.concatenate([jnp.cos(ang_r), jnp.cos(ang_r), jnp.cos(ang_c), jnp.cos(ang_c)], axis=1)
    sa = jnp.concatenate([-jnp.sin(ang_r), zeros, -jnp.sin(ang_c), zeros], axis=1)
    sb = jnp.concatenate([zeros, jnp.sin(ang_r), zeros, jnp.sin(ang_c)], axis=1)
    pad = lambda a, v: jnp.concatenate([a, jnp.full((n_ctx, DIFF_HEAD_DIM), v, F32)], axis=0)
    return pad(c, 1.0), pad(sa, 0.0), pad(sb, 0.0)


def _pack_mod(m, d):
    z = jnp.zeros((MOD_CTX_ROW - N_MOD, d), F32)
    return jnp.concatenate([m[0].reshape(N_MOD, d), z, m[1].reshape(N_MOD, d), z], axis=0)


def kernel(x, c, ctx, c_ctx, ada_w, ada_b, norm_w, pool_w, pool_b, pool_scale, attn_w_qkv, attn_w_o,
           attn_lambda, attn_subln_w, router_group_w, router_expert_w, expert_w1, expert_w3, expert_w2,
           final_norm_w):
    _, n_lat, d = x.shape
    n_ctx = ctx.shape[1]
    depth = ada_w.shape[0]
    assert x.shape[0] == 1 and n_lat % n_ctx == 0 and n_ctx % 256 == 0 and d % (4 * LANES) == 0

    mods = _adaln(c, c_ctx, ada_w, ada_b)
    mod = [_pack_mod(mods[i], d) for i in range(depth)]

    wr = jnp.concatenate([router_group_w, router_expert_w], axis=-1)
    wr = jnp.pad(wr, ((0, 0), (0, 0), (0, LANES - wr.shape[-1])))
    wr_hi = wr.astype(BF16)
    wr_lo = (wr - wr_hi.astype(F32)).astype(BF16)
    wr2 = jnp.concatenate([wr_hi, wr_lo], axis=-1)

    pool_w16 = pool_w.astype(BF16)
    wqkv16 = attn_w_qkv.astype(BF16)
    wo16 = attn_w_o.astype(BF16)
    w13_16 = _w13_call(expert_w1, expert_w3)
    w2_16 = expert_w2.astype(BF16)
    rope_c, rope_sa, rope_sb = _rope_tables(n_lat, n_ctx)

    xs, y = x[0], None
    for i in range(depth):
        jm = i // 2
        gprev = mod[i - 1] if i > 0 else None
        if i % 2 == 0:
            xs, h2, gates = _pool_call(xs, y, gprev, mod[i], norm_w[i], pool_w16, pool_b[jm],
                                       pool_scale[jm].reshape(1, d), wr2, n_lat=n_lat, layer=i, mixer=jm,
                                       ctx=ctx[0] if i == 0 else None)
        else:
            lambda_init = 0.8 - 0.6 * math.exp(-0.3 * i)
            xs, h, _ = _norm_call(xs, y, gprev, mod[i], norm_w[i, 0:1], None, k_shift=0, n_lat=n_lat)
            qk, vt = _qkv_call(h, wqkv16, jm, rope_c, rope_sa, rope_sb)
            o = _attn_call(qk, vt, attn_lambda[jm], attn_subln_w[jm], lambda_init=lambda_init,
                           n_lat=n_lat, n_ctx=n_ctx)
            xs = _oproj_call(o, wo16, jm, xs, mod[i], n_lat=n_lat)
            _, h2, gates = _norm_call(xs, None, None, mod[i], norm_w[i, 1:2], wr2, k_shift=3, n_lat=n_lat,
                                      layer=i)
        y = _moe_call(h2, gates, w13_16, w2_16, i)
    out = _final_call(xs, y, mod[depth - 1], final_norm_w.reshape(1, d), n_lat=n_lat)
    return out[None]
```

```python
import functools
import math

import jax
import jax.numpy as jnp
from jax import lax
from jax.experimental import pallas as pl
from jax.experimental.pallas import tpu as pltpu

F32 = jnp.float32
BF16 = jnp.bfloat16

GRID_W = 64
RMS_EPS = 1e-6
ROPE_THETA = 10000.0
POOL_WINDOWS = (2, 4, 8, 16)
N_MOD = 6
N_EXPERT_GROUPS = 4
EXPERTS_PER_GROUP = 4
N_EXPERTS = N_EXPERT_GROUPS * EXPERTS_PER_GROUP
DIFF_HEAD_DIM = 128
HEAD_W = 2 * DIFF_HEAD_DIM
LANES = 128
SUBLANES = 8
MXU_DIM = 256
MOD_CTX_ROW = 8
ROUTER_COL0 = N_EXPERT_GROUPS
PAIRS_PER_GROUP = 6
MOE_TILE = 128
ADALN_COLS = 2048
VMEM_LIMIT = 56 * 1024 * 1024
NEG_BIG = -1e30
LOG2E = 1.4426950408889634


def _pick(n, candidates):
    for c in candidates:
        if c <= n and n % c == 0:
            return c
    return n


def _params(semantics):
    return pltpu.CompilerParams(dimension_semantics=semantics, vmem_limit_bytes=VMEM_LIMIT)


def _layer_spec(stacked, layer):
    return pl.BlockSpec((None,) + stacked.shape[1:], lambda *_: (layer,) + (0,) * (stacked.ndim - 1))


def _silu(v):
    return v * jax.nn.sigmoid(v)


def _adaln_kernel(c_ref, cc_ref, w_ref, b_ref, o_ref, acc_ref, s_ref, *, tk, tn, nk):
    k = pl.program_id(2)

    @pl.when(k == 0)
    def _():
        acc_ref[...] = jnp.zeros_like(acc_ref)

    s_ref[0] = _silu(c_ref[...])
    s_ref[1] = _silu(cc_ref[...])

    cw = min(tn, ADALN_COLS)
    reps = cw // LANES
    for c0 in range(0, tn, cw):
        def body(j, carry, c0=c0):
            a0, a1 = carry
            r = pl.multiple_of(j * SUBLANES, SUBLANES)
            w = w_ref[pl.ds(r, SUBLANES), c0:c0 + cw]
            a0 = a0 + w * jnp.concatenate([s_ref[0, pl.ds(r, SUBLANES), :]] * reps, axis=1)
            a1 = a1 + w * jnp.concatenate([s_ref[1, pl.ds(r, SUBLANES), :]] * reps, axis=1)
            return a0, a1

        z = jnp.zeros((SUBLANES, cw), F32)
        a0, a1 = lax.fori_loop(0, tk // SUBLANES, body, (z, z), unroll=4)
        acc_ref[0, :, c0:c0 + cw] += a0
        acc_ref[1, :, c0:c0 + cw] += a1

    @pl.when(k == nk - 1)
    def _():
        o_ref[0:1, :] = jnp.sum(acc_ref[0], axis=0, keepdims=True) + b_ref[...]
        o_ref[1:2, :] = jnp.sum(acc_ref[1], axis=0, keepdims=True) + b_ref[...]


def _adaln(c, c_ctx, ada_w, ada_b):
    depth, d, n = ada_w.shape
    tk = _pick(d, (1024, 512, 256, 128))
    tn = _pick(n, (4096, 2048, 1024, 512, 256, 128))
    nk = d // tk
    c_b = jnp.broadcast_to(c.reshape(d, 1), (d, LANES))
    cc_b = jnp.broadcast_to(c_ctx.reshape(d, 1), (d, LANES))
    return pl.pallas_call(
        functools.partial(_adaln_kernel, tk=tk, tn=tn, nk=nk),
        grid=(depth, n // tn, nk),
        in_specs=[
            pl.BlockSpec((tk, LANES), lambda l, j, k: (k, 0)),
            pl.BlockSpec((tk, LANES), lambda l, j, k: (k, 0)),
            pl.BlockSpec((None, tk, tn), lambda l, j, k: (l, k, j)),
            pl.BlockSpec((None, 1, tn), lambda l, j, k: (l, 0, j)),
        ],
        out_specs=pl.BlockSpec((None, 2, tn), lambda l, j, k: (l, 0, j)),
        out_shape=jax.ShapeDtypeStruct((depth, 2, n), F32),
        scratch_shapes=[pltpu.VMEM((2, SUBLANES, tn), F32), pltpu.VMEM((2, tk, LANES), F32)],
        compiler_params=_params(("parallel", "parallel", "arbitrary")),
        name="adaln",
    )(c_b, cc_b, ada_w, ada_b.reshape(depth, 1, n))


def _mod_row(mod_ref, k, is_ctx):
    return jnp.where(is_ctx, mod_ref[MOD_CTX_ROW + k:MOD_CTX_ROW + k + 1, :], mod_ref[k:k + 1, :])


def _norm_mod(xv, w_row, shift, scale):
    r = lax.rsqrt(jnp.mean(xv * xv, axis=-1, keepdims=True) + RMS_EPS)
    return (xv * r) * (w_row * (1.0 + scale)) + shift


def _router_info(h, h_hi, wr_ref):
    h_lo = (h - h_hi.astype(F32)).astype(BF16)
    r1 = jnp.dot(h_hi, wr_ref[...], preferred_element_type=F32)
    r2 = jnp.dot(h_lo, wr_ref[:, :LANES], preferred_element_type=F32)
    logits = r1[:, :LANES] + r1[:, LANES:] + r2
    col = lax.broadcasted_iota(jnp.int32, logits.shape, 1).astype(F32)
    far = float(4 * LANES)
    is_g = col < float(N_EXPERT_GROUPS)
    mg = jnp.max(jnp.where(is_g, logits, NEG_BIG), axis=1, keepdims=True)
    denom = jnp.sum(jnp.where(is_g, jnp.exp(logits - mg), 0.0), axis=1, keepdims=True)
    pg_top = 1.0 / denom
    g_sel = jnp.min(jnp.where(is_g & (logits == mg), col, far), axis=1, keepdims=True)
    e_lo = float(ROUTER_COL0) + float(EXPERTS_PER_GROUP) * g_sel
    in_grp = (col >= e_lo) & (col < e_lo + float(EXPERTS_PER_GROUP))
    v1 = jnp.max(jnp.where(in_grp, logits, NEG_BIG), axis=1, keepdims=True)
    i1 = jnp.min(jnp.where(in_grp & (logits == v1), col, far), axis=1, keepdims=True)
    rest = in_grp & (col != i1)
    v2 = jnp.max(jnp.where(rest, logits, NEG_BIG), axis=1, keepdims=True)
    i2 = jnp.min(jnp.where(rest & (logits == v2), col, far), axis=1, keepdims=True)
    e2 = jnp.exp(v2 - v1)
    w1 = pg_top / (1.0 + e2)
    w2 = pg_top * (e2 / (1.0 + e2))
    first_lower = i1 < i2
    ja = jnp.minimum(i1, i2) - e_lo
    jb = jnp.maximum(i1, i2) - e_lo
    bucket = g_sel * float(PAIRS_PER_GROUP) + ja * (7.0 - ja) * 0.5 + (jb - ja - 1.0)
    g_lower = jnp.where(first_lower, w1, w2)
    g_higher = jnp.where(first_lower, w2, w1)
    return jnp.where(col == 0.0, bucket, jnp.where(col == 1.0, g_lower, jnp.where(col == 2.0, g_higher, 0.0)))


def _emit_routed_rows(h, wr_ref, hx_ref, info_ref):
    half = h.shape[1] // 2
    h_hi = h.astype(BF16)
    info = _router_info(h, h_hi, wr_ref)
    bits = lax.bitcast_convert_type(h_hi.astype(F32), jnp.uint32)
    word = lax.shift_right_logical(bits[:, :half], jnp.uint32(16)) | bits[:, half:]
    hx_ref[:, :half] = lax.bitcast_convert_type(word, F32)
    hx_ref[:, half:] = info
    info_ref[...] = info


def _norm_kernel(*refs, has_y, router, k_shift, lat_tiles):
    it = iter(refs)
    x_ref = next(it)
    y_ref = next(it) if has_y else None
    gmod_ref = next(it) if has_y else None
    mod_ref = next(it)
    nw_ref = next(it)
    wr_ref = next(it) if router else None
    xo_ref = next(it) if has_y else None
    h_ref = next(it)
    g_ref = next(it) if router else None

    is_ctx = pl.program_id(0) >= lat_tiles
    xv = x_ref[...]
    if has_y:
        xv = xv + _mod_row(gmod_ref, 5, is_ctx) * y_ref[...]
        xo_ref[...] = xv
    h = _norm_mod(xv, nw_ref[...], _mod_row(mod_ref, k_shift, is_ctx), _mod_row(mod_ref, k_shift + 1, is_ctx))
    if router:
        _emit_routed_rows(h, wr_ref, h_ref, g_ref)
    else:
        h_ref[...] = h.astype(BF16)


def _norm_call(x, y, gmod, mod, nw, wr, *, k_shift, n_lat, layer=0):
    t, d = x.shape
    tm = 256
    has_y = y is not None
    router = wr is not None
    row = pl.BlockSpec((tm, d), lambda i: (i, 0))
    full = lambda a: pl.BlockSpec(a.shape, lambda i: (0,) * a.ndim)
    ins, specs = [x], [row]
    if has_y:
        ins += [y, gmod]
        specs += [row, full(gmod)]
    ins += [mod, nw]
    specs += [full(mod), full(nw)]
    if router:
        ins.append(wr)
        specs.append(_layer_spec(wr, layer))
    outs, ospecs = [], []
    if has_y:
        outs.append(jax.ShapeDtypeStruct((t, d), F32))
        ospecs.append(row)
    if router:
        outs += [jax.ShapeDtypeStruct((t, d // 2 + LANES), F32), jax.ShapeDtypeStruct((t, LANES), F32)]
        ospecs += [pl.BlockSpec((tm, d // 2 + LANES), lambda i: (i, 0)), pl.BlockSpec((tm, LANES), lambda i: (i, 0))]
    else:
        outs.append(jax.ShapeDtypeStruct((t, d), BF16))
        ospecs.append(row)
    res = pl.pallas_call(
        functools.partial(_norm_kernel, has_y=has_y, router=router, k_shift=k_shift, lat_tiles=n_lat // tm),
        grid=(t // tm,),
        in_specs=specs,
        out_specs=ospecs,
        out_shape=outs,
        compiler_params=_params(("parallel",)),
        name="norm_mod",
    )(*ins)
    res = list(res)
    xo = res.pop(0) if has_y else x
    h = res.pop(0)
    g = res.pop(0) if router else None
    return xo, h, g


HALO = 8


def _pool_kernel(*refs, has_y, split_in, tp, cg, lat_tiles, n_tiles):
    it = iter(refs)
    x_ref, xp_ref, xn_ref = next(it), next(it), next(it)
    if split_in:
        c_ref, cp_ref, cn_ref = next(it), next(it), next(it)
    if has_y:
        y_ref, yp_ref, yn_ref, gmod_ref = next(it), next(it), next(it), next(it)
    mod_ref, nw_ref, pw_ref, pb_ref, ps_ref, wr_ref = (next(it) for _ in range(6))
    x1_ref, h2_ref, g_ref = next(it), next(it), next(it)
    hext_ref, xin_ref = next(it), next(it)

    j = pl.program_id(0)
    is_ctx = j >= lat_tiles
    first = (j == 0) | (j == lat_tiles)
    last = (j == lat_tiles - 1) | (j == n_tiles - 1)

    nw = nw_ref[0:1, :]
    ws = nw * (1.0 + _mod_row(mod_ref, 1, is_ctx))
    shift = _mod_row(mod_ref, 0, is_ctx)

    def x_in(xr, yr, cr):
        xv = xr[...]
        if split_in:
            xv = jnp.where(is_ctx, cr[...], xv)
        if has_y:
            xv = xv + _mod_row(gmod_ref, 5, is_ctx) * yr[...]
        return xv

    def h_of(xv):
        r = lax.rsqrt(jnp.mean(xv * xv, axis=-1, keepdims=True) + RMS_EPS)
        return (xv * r) * ws + shift

    xm = x_in(x_ref, y_ref if has_y else None, c_ref if split_in else None)
    xin_ref[...] = xm
    hext_ref[HALO:HALO + tp, :] = h_of(xm)
    hp = h_of(x_in(xp_ref, yp_ref if has_y else None, cp_ref if split_in else None))
    hext_ref[0:HALO, :] = jnp.where(first, 0.0, hp)
    hn = h_of(x_in(xn_ref, yn_ref if has_y else None, cn_ref if split_in else None))
    hext_ref[HALO + tp:HALO + tp + HALO, :] = jnp.where(last, 0.0, hn)

    hext_ref[tp + 2 * HALO:, :] = jnp.zeros((hext_ref.shape[0] - tp - 2 * HALO, hext_ref.shape[1]), F32)

    tpos = lax.broadcasted_iota(jnp.int32, (tp, 1), 0).astype(F32)
    lo_lim = jnp.where(first, 0.0, -float(2 * HALO))
    hi_lim = jnp.where(last, float(tp), float(tp + 2 * HALO))
    gate = _mod_row(mod_ref, 2, is_ctx)
    win_t = lax.broadcasted_iota(jnp.int32, (tp, hext_ref.shape[0]), 0)
    win_s = lax.broadcasted_iota(jnp.int32, (tp, hext_ref.shape[0]), 1)
    for g, w in enumerate(POOL_WINDOWS):
        c0 = g * cg
        half = w // 2
        band = jnp.where((win_s >= win_t + (HALO - half)) & (win_s < win_t + (HALO + half)), 1.0, 0.0).astype(BF16)
        hx = hext_ref[:, c0:c0 + cg]
        hx_hi = hx.astype(BF16)
        hx_lo = (hx - hx_hi.astype(F32)).astype(BF16)
        wsum = (jnp.dot(band, hx_hi, preferred_element_type=F32)
                + jnp.dot(band, hx_lo, preferred_element_type=F32))
        cnt = jnp.minimum(tpos + float(half), hi_lim) - jnp.maximum(tpos - float(half), lo_lim)
        pooled = wsum / cnt - hext_ref[HALO:HALO + tp, c0:c0 + cg]
        yg = jnp.dot(pooled.astype(BF16), pw_ref[g], preferred_element_type=F32) + pb_ref[g:g + 1, :]
        yg = yg * ps_ref[:, c0:c0 + cg]
        x1_ref[:, c0:c0 + cg] = xin_ref[:, c0:c0 + cg] + gate[:, c0:c0 + cg] * yg

    x1 = x1_ref[...]
    h2 = _norm_mod(x1, nw_ref[1:2, :], _mod_row(mod_ref, 3, is_ctx), _mod_row(mod_ref, 4, is_ctx))
    _emit_routed_rows(h2, wr_ref, h2_ref, g_ref)


def _pool_call(x, y, gmod, mod, nw2, pw, pb, ps, wr, *, n_lat, layer, mixer, ctx=None):
    d = x.shape[1]
    split_in = ctx is not None
    t = x.shape[0] + (ctx.shape[0] if split_in else 0)
    tp = 128
    cg = d // len(POOL_WINDOWS)
    has_y = y is not None
    n_tiles = t // tp
    lat_tiles = n_lat // tp
    bpt = tp // HALO

    def halo_specs(n_rows, tile0):
        n_t, last_blk = n_rows // tp, n_rows // HALO - 1
        clip = lambda v, hi: jnp.minimum(jnp.maximum(v, 0), hi)
        return [pl.BlockSpec((tp, d), lambda i: (clip(i - tile0, n_t - 1), 0)),
                pl.BlockSpec((HALO, d), lambda i: (clip((i - tile0) * bpt - 1, last_blk), 0)),
                pl.BlockSpec((HALO, d), lambda i: (clip((i - tile0 + 1) * bpt, last_blk), 0))]

    row = pl.BlockSpec((tp, d), lambda i: (i, 0))
    full = lambda a: pl.BlockSpec(a.shape, lambda i: (0,) * a.ndim)
    ins, specs = [x, x, x], halo_specs(x.shape[0], 0)
    if split_in:
        ins += [ctx, ctx, ctx]
        specs += halo_specs(ctx.shape[0], lat_tiles)
    _, prev, nxt = halo_specs(t, 0)
    if has_y:
        ins += [y, y, y, gmod]
        specs += [row, prev, nxt, full(gmod)]
    ins += [mod, nw2, pw, pb, ps, wr]
    specs += [full(mod), full(nw2), _layer_spec(pw, mixer), full(pb), full(ps), _layer_spec(wr, layer)]
    return pl.pallas_call(
        functools.partial(_pool_kernel, has_y=has_y, split_in=split_in, tp=tp, cg=cg, lat_tiles=lat_tiles,
                          n_tiles=n_tiles),
        grid=(n_tiles,),
        in_specs=specs,
        out_specs=[row, pl.BlockSpec((tp, d // 2 + LANES), lambda i: (i, 0)),
                   pl.BlockSpec((tp, LANES), lambda i: (i, 0))],
        out_shape=[jax.ShapeDtypeStruct((t, d), F32), jax.ShapeDtypeStruct((t, d // 2 + LANES), F32),
                   jax.ShapeDtypeStruct((t, LANES), F32)],
        scratch_shapes=[pltpu.VMEM((-(-(tp + 2 * HALO) // MXU_DIM) * MXU_DIM, d), F32), pltpu.VMEM((tp, d), F32)],
        compiler_params=_params(("parallel",)),
        name="pool_layer",
    )(*ins)


def _qk_kernel(a_ref, w_ref, c_ref, sa_ref, sb_ref, o_ref, *, d, tn, qscale):
    acc = jnp.dot(a_ref[...], w_ref[...], preferred_element_type=F32)
    f = jnp.where(pl.program_id(1) < d // tn, qscale, 1.0)
    cc = c_ref[...] * f
    sa = sa_ref[...] * f
    sb = sb_ref[...] * f
    for b in range(tn // LANES):
        blk = acc[:, b * LANES:(b + 1) * LANES]
        rot = blk * cc + pltpu.roll(blk, 96, 1) * sa + pltpu.roll(blk, 32, 1) * sb
        o_ref[:, b * LANES:(b + 1) * LANES] = rot.astype(BF16)


def _vt_kernel(a_ref, w_ref, o_ref, acc_ref):
    acc_ref[...] = jnp.dot(a_ref[...], w_ref[...], preferred_element_type=F32)
    o_ref[...] = acc_ref[...].T.astype(BF16)


def _qkv_call(h, w, mixer, rope_c, rope_sa, rope_sb):
    t, d = h.shape
    tm = _pick(t, (640, 512, 256, 128))
    tn = _pick(d, (1024, 512, 256))
    qscale = (DIFF_HEAD_DIM ** -0.5) * LOG2E
    tab = pl.BlockSpec((tm, LANES), lambda i, j: (i, 0))
    a_spec = pl.BlockSpec((tm, d), lambda i, j: (i, 0))
    qk = pl.pallas_call(
        functools.partial(_qk_kernel, d=d, tn=tn, qscale=qscale),
        grid=(t // tm, (2 * d) // tn),
        in_specs=[a_spec, pl.BlockSpec((None, d, tn), lambda i, j: (mixer, 0, j)), tab, tab, tab],
        out_specs=pl.BlockSpec((tm, tn), lambda i, j: (i, j)),
        out_shape=jax.ShapeDtypeStruct((t, 2 * d), BF16),
        compiler_params=_params(("parallel", "arbitrary")),
        name="qk_proj",
    )(h, w, rope_c, rope_sa, rope_sb)
    v_col0 = (2 * d) // tn
    vt = pl.pallas_call(
        _vt_kernel,
        grid=(t // tm, d // tn),
        in_specs=[a_spec, pl.BlockSpec((None, d, tn), lambda i, j: (mixer, 0, v_col0 + j))],
        out_specs=pl.BlockSpec((tn, tm), lambda i, j: (j, i)),
        out_shape=jax.ShapeDtypeStruct((d, t), BF16),
        scratch_shapes=[pltpu.VMEM((tm, tn), F32)],
        compiler_params=_params(("parallel", "arbitrary")),
        name="v_proj_t",
    )(h, w)
    return qk, vt


def _oproj_kernel(a_ref, w_ref, x_ref, mod_ref, o_ref, *, tm, n_lat):
    acc = jnp.dot(a_ref[...], w_ref[...], preferred_element_type=F32)
    rows = pl.program_id(0) * tm + lax.broadcasted_iota(jnp.int32, (tm, 1), 0)
    gate = jnp.where(rows >= n_lat, mod_ref[MOD_CTX_ROW + 2:MOD_CTX_ROW + 3, :], mod_ref[2:3, :])
    o_ref[...] = x_ref[...] + gate * acc


def _oproj_call(o, w, mixer, x, mod, *, n_lat):
    t, d = x.shape
    tm = _pick(t, (640, 512, 256, 128))
    tn = _pick(d, (1024, 512, 256))
    return pl.pallas_call(
        functools.partial(_oproj_kernel, tm=tm, n_lat=n_lat),
        grid=(t // tm, d // tn),
        in_specs=[pl.BlockSpec((tm, d), lambda i, j: (i, 0)),
                  pl.BlockSpec((None, d, tn), lambda i, j: (mixer, 0, j)),
                  pl.BlockSpec((tm, tn), lambda i, j: (i, j)), pl.BlockSpec((16, tn), lambda i, j: (0, j))],
        out_specs=pl.BlockSpec((tm, tn), lambda i, j: (i, j)),
        out_shape=jax.ShapeDtypeStruct((t, d), F32),
        compiler_params=_params(("parallel", "arbitrary")),
        name="attn_out_proj",
    )(o, w, x, mod)


def _diff_lambda(lam_ref, lambda_init):
    lv = lam_ref[...]
    a = jnp.sum(lv[0:1, :] * lv[1:2, :], axis=1, keepdims=True)
    b = jnp.sum(lv[2:3, :] * lv[3:4, :], axis=1, keepdims=True)
    return jnp.exp(a) - jnp.exp(b) + lambda_init


def _attn_scores_t(q_ref, k_ref, c):
    qc = q_ref[:, c * DIFF_HEAD_DIM:(c + 1) * DIFF_HEAD_DIM]
    kc = k_ref[:, c * DIFF_HEAD_DIM:(c + 1) * DIFF_HEAD_DIM]
    return lax.dot_general(kc, qc, (((1,), (1,)), ((), ())), preferred_element_type=F32)


def _attn_finish(ot, sw_ref, lambda_init):
    r = lax.rsqrt(jnp.mean(ot * ot, axis=0, keepdims=True) + RMS_EPS)
    return (((ot * r).T * sw_ref[...]) * (1.0 - lambda_init)).astype(BF16)


def _attn_kernel(q_ref, k_ref, vt_ref, octx_ref, lam_ref, sw_ref, o_ref, m_ref, l_ref, acc_ref,
                 *, lambda_init, nq, nk, qp):
    qi = pl.program_id(1)
    ki = pl.program_id(2)

    @pl.when(qi < nq)
    def _():
        _attn_step(q_ref, k_ref, vt_ref, lam_ref, sw_ref, o_ref, m_ref, l_ref, acc_ref, ki,
                   lambda_init=lambda_init, nk=nk, qp=qp)

    @pl.when((qi == nq) & (ki == nk - 1))
    def _():
        n_ctx = octx_ref.shape[0]
        o_ref[0:n_ctx, :] = octx_ref[...]
        o_ref[n_ctx:, :] = jnp.zeros((o_ref.shape[0] - n_ctx, o_ref.shape[1]), o_ref.dtype)


def _attn_step(q_ref, k_ref, vt_ref, lam_ref, sw_ref, o_ref, m_ref, l_ref, acc_ref, ki, *, lambda_init, nk, qp):
    @pl.when(ki == 0)
    def _():
        m_ref[...] = jnp.full(m_ref.shape, NEG_BIG, F32)
        l_ref[...] = jnp.zeros_like(l_ref)
        acc_ref[...] = jnp.zeros_like(acc_ref)

    tq = q_ref.shape[0]
    chains = [(c, j) for j in range(tq // qp) for c in range(2)]

    def scores(c, j):
        qc = q_ref[j * qp:(j + 1) * qp, c * DIFF_HEAD_DIM:(c + 1) * DIFF_HEAD_DIM]
        kc = k_ref[:, c * DIFF_HEAD_DIM:(c + 1) * DIFF_HEAD_DIM]
        return lax.dot_general(kc, qc, (((1,), (1,)), ((), ())), preferred_element_type=F32)

    s_next = scores(*chains[0])
    for n, (c, j) in enumerate(chains):
        s = s_next
        if n + 1 < len(chains):
            s_next = scores(*chains[n + 1])
        cols = slice(j * qp, (j + 1) * qp)
        m_prev = m_ref[c, :, cols]
        m_new = jnp.maximum(m_prev, jnp.max(s, axis=0, keepdims=True))
        alpha = jnp.exp2(m_prev - m_new)
        p = jnp.exp2(s - m_new)
        l_ref[c, :, cols] = alpha * l_ref[c, :, cols] + jnp.sum(p, axis=0, keepdims=True)
        acc_ref[c, :, cols] = alpha * acc_ref[c, :, cols] + jnp.dot(
            vt_ref[...], p.astype(BF16), preferred_element_type=F32)
        m_ref[c, :, cols] = m_new

    @pl.when(ki == nk - 1)
    def _():
        lam = _diff_lambda(lam_ref, lambda_init)
        ot = acc_ref[0] / l_ref[0] - lam * (acc_ref[1] / l_ref[1])
        o_ref[...] = _attn_finish(ot, sw_ref, lambda_init)


def _attn_ctx_kernel(q_ref, k_ref, vt_ref, lam_ref, sw_ref, o_ref, *, lambda_init):
    lam = _diff_lambda(lam_ref, lambda_init)
    outs = []
    for c in range(2):
        s = _attn_scores_t(q_ref, k_ref, c)
        p = jnp.exp2(s - jnp.max(s, axis=0, keepdims=True))
        l = jnp.sum(p, axis=0, keepdims=True)
        outs.append(jnp.dot(vt_ref[...], p.astype(BF16), preferred_element_type=F32) / l)
    o_ref[...] = _attn_finish(outs[0] - lam * outs[1], sw_ref, lambda_init)


def _attn_call(qk, vt, lam_vecs, subln_w, *, lambda_init, n_lat, n_ctx):
    d, t = vt.shape
    heads = d // HEAD_W
    tq = _pick(n_lat, (4096, 2048, 1024, 512, 256, 128))
    tk = _pick(t, (3328, 1280, 640, 256, 128))
    nq = n_lat // tq
    nk = t // tk
    cblk = n_lat // n_ctx
    assert n_ctx <= tq
    sw = subln_w.reshape(1, HEAD_W)
    small = lambda a: pl.BlockSpec(a.shape, lambda *_: (0,) * a.ndim)
    o_ctx = pl.pallas_call(
        functools.partial(_attn_ctx_kernel, lambda_init=lambda_init),
        grid=(heads,),
        in_specs=[
            pl.BlockSpec((n_ctx, HEAD_W), lambda h: (cblk, h)),
            pl.BlockSpec((n_ctx, HEAD_W), lambda h: (cblk, heads + h)),
            pl.BlockSpec((HEAD_W, n_ctx), lambda h: (h, cblk)),
            small(lam_vecs), small(sw),
        ],
        out_specs=pl.BlockSpec((n_ctx, HEAD_W), lambda h: (0, h)),
        out_shape=jax.ShapeDtypeStruct((n_ctx, d), BF16),
        compiler_params=_params(("parallel",)),
        name="diff_attn_ctx",
    )(qk, qk, vt, lam_vecs, sw)
    last_q = nq - 1
    kv_blk = lambda i, k: jnp.where(i < nq, k, nk - 1)
    return pl.pallas_call(
        functools.partial(_attn_kernel, lambda_init=lambda_init, nq=nq, nk=nk, qp=min(tq, 256)),
        grid=(heads, nq + 1, nk),
        in_specs=[
            pl.BlockSpec((tq, HEAD_W), lambda h, i, k: (jnp.minimum(i, last_q), h)),
            pl.BlockSpec((tk, HEAD_W), lambda h, i, k: (kv_blk(i, k), heads + h)),
            pl.BlockSpec((HEAD_W, tk), lambda h, i, k: (h, kv_blk(i, k))),
            pl.BlockSpec((n_ctx, HEAD_W), lambda h, i, k: (0, h)),
            small(lam_vecs), small(sw),
        ],
        out_specs=pl.BlockSpec((tq, HEAD_W), lambda h, i, k: (i, h)),
        out_shape=jax.ShapeDtypeStruct(((nq + 1) * tq, d), BF16),
        scratch_shapes=[pltpu.VMEM((2, 1, tq), F32), pltpu.VMEM((2, 1, tq), F32),
                        pltpu.VMEM((2, HEAD_W, tq), F32)],
        compiler_params=_params(("parallel", "parallel", "arbitrary")),
        name="diff_attn",
    )(qk, qk, vt, o_ctx, lam_vecs, sw)


def _moe_kernel(elo_ref, ehi_ref, src_ref, srcn_ref, dst_ref, x_hbm, w13l_ref, w13h_ref, w2l_ref, w2h_ref,
                y_hbm, xbuf, ybuf, gsem, ssem, *, ts, d, n_tiles):
    del elo_ref, ehi_ref
    i = pl.program_id(0)
    slot = i % 2
    nslot = 1 - slot

    def start_gather(idx_ref, sl):
        for r in range(ts):
            pltpu.make_async_copy(x_hbm.at[pl.ds(idx_ref[0, r], 1)], xbuf.at[sl, pl.ds(r, 1)],
                                  gsem.at[sl]).start(priority=r % 2)

    def wait_gather(sl):
        pltpu.make_async_copy(x_hbm.at[pl.ds(0, ts)], xbuf.at[sl], gsem.at[sl]).wait()

    def wait_scatter(sl):
        pltpu.make_async_copy(ybuf.at[sl], y_hbm.at[pl.ds(0, ts)], ssem.at[sl]).wait()

    @pl.when(i == 0)
    def _():
        start_gather(src_ref, 0)

    wait_gather(slot)

    @pl.when(i >= 2)
    def _():
        wait_scatter(slot)

    start_gather(srcn_ref, nslot)

    xin = xbuf[slot]
    half = d // 2
    word = lax.bitcast_convert_type(xin[:, :half], jnp.uint32)
    h_a = lax.bitcast_convert_type(lax.shift_left(word, jnp.uint32(16)), F32).astype(BF16)
    h_b = lax.bitcast_convert_type(word & jnp.uint32(0xFFFF0000), F32).astype(BF16)
    g_lower = xin[:, half + 1:half + 2]
    g_higher = xin[:, half + 2:half + 3]
    f = w2l_ref.shape[0]

    def hidden(w13_ref, gate):
        au = (jnp.dot(h_a, w13_ref[:half, :], preferred_element_type=F32)
              + jnp.dot(h_b, w13_ref[half:, :], preferred_element_type=F32))
        return ((_silu(au[:, :f]) * au[:, f:]) * gate).astype(BF16)

    ybuf[slot] = (jnp.dot(hidden(w13l_ref, g_lower), w2l_ref[...], preferred_element_type=F32)
                  + jnp.dot(hidden(w13h_ref, g_higher), w2h_ref[...], preferred_element_type=F32))

    for r in range(ts):
        pltpu.make_async_copy(ybuf.at[slot, pl.ds(r, 1)], y_hbm.at[pl.ds(dst_ref[0, r], 1)],
                              ssem.at[slot]).start(priority=r % 2)

    @pl.when(i == n_tiles - 1)
    def _():
        wait_scatter(slot)
        wait_scatter(nslot)
        wait_gather(nslot)


def _w13_kernel(w1_ref, w3_ref, o_ref):
    f = w1_ref.shape[-1]
    o_ref[:, :f] = w1_ref[...].astype(BF16)
    o_ref[:, f:] = w3_ref[...].astype(BF16)


def _w13_call(w1, w3):
    depth, n_e, d, f = w1.shape
    tr = _pick(d, (2048, 1024, 512, 256))
    spec_in = pl.BlockSpec((None, None, tr, f), lambda l, e, r: (l, e, r, 0))
    return pl.pallas_call(
        _w13_kernel,
        grid=(depth, n_e, d // tr),
        in_specs=[spec_in, spec_in],
        out_specs=pl.BlockSpec((None, None, tr, 2 * f), lambda l, e, r: (l, e, r, 0)),
        out_shape=jax.ShapeDtypeStruct((depth, n_e, d, 2 * f), BF16),
        compiler_params=_params(("parallel", "parallel", "parallel")),
        name="w13_pack",
    )(w1, w3)


def _moe_plan(info, *, ts, n_pad_rows):
    t = info.shape[0]
    n_buckets = N_EXPERT_GROUPS * PAIRS_PER_GROUP
    raw = info[:, 0].astype(jnp.int32)
    bucket = (raw // PAIRS_PER_GROUP) * PAIRS_PER_GROUP + jnp.array([0, 1, 4, 2, 3, 5], jnp.int32)[raw % PAIRS_PER_GROUP]
    onehot = (bucket[:, None] == jnp.arange(n_buckets, dtype=jnp.int32)[None, :]).astype(jnp.int32)
    csum = jnp.cumsum(onehot, axis=0)
    rank = jnp.sum(csum * onehot, axis=1) - 1
    counts = csum[-1]
    padded = ((counts + ts - 1) // ts) * ts
    ends = jnp.cumsum(padded)
    starts = ends - padded
    pos = jnp.sum(starts[None, :] * onehot, axis=1) + rank
    tok = jnp.arange(t, dtype=jnp.int32)
    slot_tok = jnp.full((n_pad_rows,), -1, jnp.int32).at[pos].set(tok)
    valid = slot_tok >= 0
    src = jnp.maximum(slot_tok, 0)
    pad_rank = jnp.cumsum(1 - valid.astype(jnp.int32)) - 1
    dst = jnp.where(valid, slot_tok, t + pad_rank).astype(jnp.int32)
    n_tiles = n_pad_rows // ts
    tile_bucket = jnp.sum((jnp.arange(n_tiles, dtype=jnp.int32)[:, None] * ts >= ends[None, :]).astype(jnp.int32),
                          axis=1)
    tile_bucket = jnp.minimum(tile_bucket, n_buckets - 1)
    pair = tile_bucket % PAIRS_PER_GROUP
    pair_lo = jnp.array([0, 0, 1, 1, 0, 2], jnp.int32)[pair]
    pair_hi = jnp.array([1, 2, 2, 3, 3, 3], jnp.int32)[pair]
    base = (tile_bucket // PAIRS_PER_GROUP) * EXPERTS_PER_GROUP
    return (src.reshape(n_tiles, 1, ts), dst.reshape(n_tiles, 1, ts),
            (base + pair_lo).astype(jnp.int32), (base + pair_hi).astype(jnp.int32))


def _moe_call(hx, info, w13, w2, layer):
    t, dx = hx.shape
    d = 2 * (dx - LANES)
    f = w2.shape[2]
    ts = MOE_TILE
    n_pad_rows = t + N_EXPERT_GROUPS * PAIRS_PER_GROUP * ts
    n_tiles = n_pad_rows // ts
    src, dst, e_lo, e_hi = _moe_plan(info, ts=ts, n_pad_rows=n_pad_rows)
    smem_tile = lambda fn: pl.BlockSpec((None, 1, ts), fn, memory_space=pltpu.SMEM)
    w_in = lambda pick: pl.BlockSpec((None, None, d, 2 * f), lambda i, lo, hi: (layer, pick(lo, hi)[i], 0, 0))
    w_out = lambda pick: pl.BlockSpec((None, None, f, d), lambda i, lo, hi: (layer, pick(lo, hi)[i], 0, 0))
    lower = lambda lo, hi: lo
    higher = lambda lo, hi: hi
    grid_spec = pltpu.PrefetchScalarGridSpec(
        num_scalar_prefetch=2,
        grid=(n_tiles,),
        in_specs=[
            smem_tile(lambda i, lo, hi: (i, 0, 0)),
            smem_tile(lambda i, lo, hi: (jnp.minimum(i + 1, n_tiles - 1), 0, 0)),
            smem_tile(lambda i, lo, hi: (i, 0, 0)),
            pl.BlockSpec(memory_space=pl.ANY),
            w_in(lower), w_in(higher), w_out(lower), w_out(higher),
        ],
        out_specs=pl.BlockSpec(memory_space=pl.ANY),
        scratch_shapes=[pltpu.VMEM((2, ts, dx), F32), pltpu.VMEM((2, ts, d), F32),
                        pltpu.SemaphoreType.DMA((2,)), pltpu.SemaphoreType.DMA((2,))],
    )
    return pl.pallas_call(
        functools.partial(_moe_kernel, ts=ts, d=d, n_tiles=n_tiles),
        grid_spec=grid_spec,
        out_shape=jax.ShapeDtypeStruct((n_pad_rows, d), F32),
        compiler_params=_params(("arbitrary",)),
        name="moe_routed",
    )(e_lo, e_hi, src, src, dst, hx, w13, w13, w2, w2)


def _final_kernel(x_ref, y_ref, gmod_ref, fw_ref, o_ref):
    xv = x_ref[...] + gmod_ref[5:6, :] * y_ref[...]
    r = lax.rsqrt(jnp.mean(xv * xv, axis=-1, keepdims=True) + RMS_EPS)
    o_ref[...] = (xv * r) * fw_ref[...]


def _final_call(x, y, gmod, fw, *, n_lat):
    _, d = x.shape
    tm = 256
    row = pl.BlockSpec((tm, d), lambda i: (i, 0))
    full = lambda a: pl.BlockSpec(a.shape, lambda i: (0,) * a.ndim)
    return pl.pallas_call(
        _final_kernel,
        grid=(n_lat // tm,),
        in_specs=[row, row, full(gmod), full(fw)],
        out_specs=row,
        out_shape=jax.ShapeDtypeStruct((n_lat, d), F32),
        compiler_params=_params(("parallel",)),
        name="final_norm",
    )(x, y, gmod, fw)


def _rope_tables(n_lat, n_ctx):
    axis_dim = DIFF_HEAD_DIM // 2
    quarter = axis_dim // 2
    rows = n_lat // GRID_W
    row = jnp.broadcast_to(jnp.arange(rows, dtype=F32)[:, None], (rows, GRID_W)).reshape(-1)
    col = jnp.broadcast_to(jnp.arange(GRID_W, dtype=F32)[None, :], (rows, GRID_W)).reshape(-1)
    inv_freq = ROPE_THETA ** (-jnp.arange(0, axis_dim, 2, dtype=F32) / axis_dim)
    ang_r = row[:, None] * inv_freq
    ang_c = col[:, None] * inv_freq
    zeros = jnp.zeros((n_lat, quarter), F32)
    c = jnp.concatenate([jnp.cos(ang_r), jnp.cos(ang_r), jnp.cos(ang_c), jnp.cos(ang_c)], axis=1)
    sa = jnp.concatenate([-jnp.sin(ang_r), zeros, -jnp.sin(ang_c), zeros], axis=1)
    sb = jnp.concatenate([zeros, jnp.sin(ang_r), zeros, jnp.sin(ang_c)], axis=1)
    pad = lambda a, v: jnp.concatenate([a, jnp.full((n_ctx, DIFF_HEAD_DIM), v, F32)], axis=0)
    return pad(c, 1.0), pad(sa, 0.0), pad(sb, 0.0)


def _pack_mod(m, d):
    z = jnp.zeros((MOD_CTX_ROW - N_MOD, d), F32)
    return jnp.concatenate([m[0].reshape(N_MOD, d), z, m[1].reshape(N_MOD, d), z], axis=0)


def kernel(x, c, ctx, c_ctx, ada_w, ada_b, norm_w, pool_w, pool_b, pool_scale, attn_w_qkv, attn_w_o,
           attn_lambda, attn_subln_w, router_group_w, router_expert_w, expert_w1, expert_w3, expert_w2,
           final_norm_w):
    _, n_lat, d = x.shape
    n_ctx = ctx.shape[1]
    depth = ada_w.shape[0]
    assert x.shape[0] == 1 and n_lat % n_ctx == 0 and n_ctx % 256 == 0 and d % (4 * LANES) == 0

    mods = _adaln(c, c_ctx, ada_w, ada_b)
    mod = [_pack_mod(mods[i], d) for i in range(depth)]

    wr = jnp.concatenate([router_group_w, router_expert_w], axis=-1)
    wr = jnp.pad(wr, ((0, 0), (0, 0), (0, LANES - wr.shape[-1])))
    wr_hi = wr.astype(BF16)
    wr_lo = (wr - wr_hi.astype(F32)).astype(BF16)
    wr2 = jnp.concatenate([wr_hi, wr_lo], axis=-1)

    pool_w16 = pool_w.astype(BF16)
    wqkv16 = attn_w_qkv.astype(BF16)
    wo16 = attn_w_o.astype(BF16)
    w13_16 = _w13_call(expert_w1, expert_w3)
    w2_16 = expert_w2.astype(BF16)
    rope_c, rope_sa, rope_sb = _rope_tables(n_lat, n_ctx)

    xs, y = x[0], None
    for i in range(depth):
        jm = i // 2
        gprev = mod[i - 1] if i > 0 else None
        if i % 2 == 0:
            xs, h2, gates = _pool_call(xs, y, gprev, mod[i], norm_w[i], pool_w16, pool_b[jm],
                                       pool_scale[jm].reshape(1, d), wr2, n_lat=n_lat, layer=i, mixer=jm,
                                       ctx=ctx[0] if i == 0 else None)
        else:
            lambda_init = 0.8 - 0.6 * math.exp(-0.3 * i)
            xs, h, _ = _norm_call(xs, y, gprev, mod[i], norm_w[i, 0:1], None, k_shift=0, n_lat=n_lat)
            qk, vt = _qkv_call(h, wqkv16, jm, rope_c, rope_sa, rope_sb)
            o = _attn_call(qk, vt, attn_lambda[jm], attn_subln_w[jm], lambda_init=lambda_init,
                           n_lat=n_lat, n_ctx=n_ctx)
            xs = _oproj_call(o, wo16, jm, xs, mod[i], n_lat=n_lat)
            _, h2, gates = _norm_call(xs, None, None, mod[i], norm_w[i, 1:2], wr2, k_shift=3, n_lat=n_lat,
                                      layer=i)
        y = _moe_call(h2, gates, w13_16, w2_16, i)
    out = _final_call(xs, y, mod[depth - 1], final_norm_w.reshape(1, d), n_lat=n_lat)
    return out[None]
```
